```python
import math
import jax, jax.numpy as jnp
from jax import lax
import numpy as np

D_MODEL = 4096
BATCH = 2
SEQ = 4096
DEPTH = 2

N_MIXERS = 2
N_LAYERS_A = (DEPTH + 1) // 2
N_LAYERS_B = DEPTH // 2

A_DK = 128
A_HEADS = D_MODEL // A_DK
A_DV = D_MODEL // A_HEADS
A_KEY_WIDTH = A_HEADS * A_DK
A_VAL_WIDTH = A_HEADS * A_DV
A_IN_WIDTH = 2 * A_KEY_WIDTH + 2 * A_VAL_WIDTH

B_DK = 128
B_DV = 128
B_HEADS = D_MODEL // B_DV
B_KEY_WIDTH = B_HEADS * B_DK
B_VAL_WIDTH = B_HEADS * B_DV
B_CONV_WIDTH = 4
B_CONV_CH = 2 * B_KEY_WIDTH + B_VAL_WIDTH
B_IN_WIDTH = B_CONV_CH + B_VAL_WIDTH + 2 * B_HEADS

CHUNK = 64
DEEPNORM_ALPHA = (2.0 * DEPTH) ** 0.25
DEEPNORM_BETA = (8.0 * DEPTH) ** -0.25
LN_EPS = 1e-5
RMS_EPS = 1e-6
L2_EPS = 1e-6

kernel_name = "hybrid_hgrn2_gated_deltanet_deepnorm"


def layer_norm(x, g, b):
    xf = x.astype(jnp.float32)
    mu = jnp.mean(xf, axis=-1, keepdims=True)
    var = jnp.mean(jnp.square(xf - mu), axis=-1, keepdims=True)
    return ((xf - mu) * lax.rsqrt(var + LN_EPS) * g.astype(jnp.float32) + b.astype(jnp.float32)).astype(x.dtype)


def gated_rmsnorm(o, z, w):
    b, s, h, dv = o.shape
    on = o * lax.rsqrt(jnp.mean(o * o, axis=-1, keepdims=True) + RMS_EPS) * w.astype(jnp.float32)
    return on.reshape(b, s, h * dv) * jax.nn.silu(z.astype(jnp.float32))


def to_chunks(x):
    b, s, h, d = x.shape
    return x.reshape(b, s // CHUNK, CHUNK, h, d).transpose(1, 0, 3, 2, 4)


def from_chunks(y):
    n, b, h, c, d = y.shape
    return y.transpose(1, 0, 3, 2, 4).reshape(b, n * c, h, d)


def causal_depthwise_conv(x, w):
    k = w.shape[0]
    return lax.conv_general_dilated(x, w[:, None, :], window_strides=(1,), padding=[(k - 1, 0)],
                                    dimension_numbers=("NWC", "WIO", "NWC"),
                                    feature_group_count=x.shape[-1])


def hgrn2_chunk_scan(q, k, v, log_f):
    n, b, h, c, dk = q.shape
    dv = v.shape[-1]
    causal = jnp.tril(jnp.ones((c, c), dtype=bool))

    def step(state, xs):
        q_c, k_c, v_c, g_c = xs
        cum = jnp.cumsum(g_c, axis=-2)
        diff = cum[..., :, None, :] - cum[..., None, :, :]
        decay = jnp.exp(jnp.where(causal[:, :, None], diff, -jnp.inf))
        attn = jnp.einsum("bhtd,bhsd,bhtsd->bhts", q_c, k_c, decay)
        o = (jnp.einsum("bhts,bhsv->bhtv", attn, v_c)
             + jnp.einsum("bhtd,bhdv->bhtv", q_c * jnp.exp(cum), state))
        last = cum[..., -1:, :]
        state = (jnp.exp(last[..., 0, :])[..., None] * state
                 + jnp.einsum("bhsd,bhsv->bhdv", k_c * jnp.exp(last - cum), v_c))
        return state, o

    state0 = jnp.zeros((b, h, dk, dv), jnp.float32)
    _, o = lax.scan(step, state0, (q, k, v, log_f))
    return o


def hgrn2_mixer(x, w_in, lb, norm_w, w_out):
    b, s, _ = x.shape
    hproj = x @ w_in
    q, f, i, z = jnp.split(hproj, [A_KEY_WIDTH, 2 * A_KEY_WIDTH, 2 * A_KEY_WIDTH + A_VAL_WIDTH], axis=-1)
    q = jax.nn.silu(q.astype(jnp.float32))
    forget = lb + (1.0 - lb) * jax.nn.sigmoid(f.astype(jnp.float32))
    k = 1.0 - forget
    log_f = jnp.log(forget)
    heads_k = lambda t: to_chunks(t.reshape(b, s, A_HEADS, A_DK))
    o = hgrn2_chunk_scan(heads_k(q), heads_k(k),
                         to_chunks(i.astype(jnp.float32).reshape(b, s, A_HEADS, A_DV)),
                         heads_k(log_f))
    y = gated_rmsnorm(from_chunks(o), z, norm_w)
    return y.astype(x.dtype) @ w_out


def gated_delta_chunk(q, k, v, beta, g):
    c = q.shape[-2]
    dv = v.shape[-1]
    strict = jnp.tril(jnp.ones((c, c), dtype=bool), -1)
    incl = jnp.tril(jnp.ones((c, c), dtype=bool))
    cum = jnp.cumsum(g, axis=-1)
    diff = cum[..., :, None] - cum[..., None, :]
    kk = jnp.einsum("nbhtd,nbhsd->nbhts", k, k)
    lower = beta[..., :, None] * kk * jnp.exp(jnp.where(strict, diff, -jnp.inf))
    eye = jnp.eye(c, dtype=jnp.float32)
    rhs = jnp.concatenate([beta[..., None] * v, (beta * jnp.exp(cum))[..., None] * k], axis=-1)
    sol = lax.linalg.triangular_solve(eye + lower, rhs, left_side=True, lower=True, unit_diagonal=True)
    u0, w = sol[..., :dv], sol[..., dv:]
    qk = jnp.einsum("nbhtd,nbhsd->nbhts", q, k) * jnp.exp(jnp.where(incl, diff, -jnp.inf))
    q_decay = q * jnp.exp(cum)[..., None]
    k_decay = k * jnp.exp(cum[..., -1:] - cum)[..., None]
    last_decay = jnp.exp(cum[..., -1])

    def step(state, xs):
        u0_c, w_c, qk_c, qd_c, kd_c, ld_c = xs
        u = u0_c - jnp.einsum("bhcd,bhdv->bhcv", w_c, state)
        o = jnp.einsum("bhcd,bhdv->bhcv", qd_c, state) + jnp.einsum("bhts,bhsv->bhtv", qk_c, u)
        state = ld_c[..., None, None] * state + jnp.einsum("bhcd,bhcv->bhdv", kd_c, u)
        return state, o

    n, b, h, _, dk = q.shape
    state0 = jnp.zeros((b, h, dk, dv), jnp.float32)
    _, o = lax.scan(step, state0, (u0, w, qk, q_decay, k_decay, last_decay))
    return o


def l2_normalize(t):
    return t * lax.rsqrt(jnp.sum(t * t, axis=-1, keepdims=True) + L2_EPS)


def gated_deltanet_mixer(x, w_in, conv_w, a_log, dt_bias, norm_w, w_out):
    b, s, _ = x.shape
    hproj = x @ w_in
    qkv, z, beta_logit, a = jnp.split(
        hproj, [B_CONV_CH, B_CONV_CH + B_VAL_WIDTH, B_CONV_CH + B_VAL_WIDTH + B_HEADS], axis=-1)
    qkv = jax.nn.silu(causal_depthwise_conv(qkv, conv_w).astype(jnp.float32))
    q, k, v = jnp.split(qkv, [B_KEY_WIDTH, 2 * B_KEY_WIDTH], axis=-1)
    q = l2_normalize(q.reshape(b, s, B_HEADS, B_DK)) * (B_DK ** -0.5)
    k = l2_normalize(k.reshape(b, s, B_HEADS, B_DK))
    v = v.reshape(b, s, B_HEADS, B_DV)
    beta = jax.nn.sigmoid(beta_logit.astype(jnp.float32))
    g = -jnp.exp(a_log.astype(jnp.float32)) * jax.nn.softplus(
        a.astype(jnp.float32) + dt_bias.astype(jnp.float32))
    o = gated_delta_chunk(to_chunks(q), to_chunks(k), to_chunks(v),
                          to_chunks(beta[..., None])[..., 0], to_chunks(g[..., None])[..., 0])
    y = gated_rmsnorm(from_chunks(o), z, norm_w)
    return y.astype(x.dtype) @ w_out


def setup_inputs(seed: int = 0) -> dict:
    key = jax.random.key(seed)
    ks = jax.random.split(key, 16)
    f32 = jnp.float32
    D = D_MODEL
    x = jax.random.normal(ks[0], (BATCH, SEQ, D), f32)
    a_w_in = jax.random.normal(ks[1], (N_LAYERS_A, D, A_IN_WIDTH), f32) * (D ** -0.5)
    a_lower_bounds = jax.random.normal(ks[2], (DEPTH + 1, A_KEY_WIDTH), f32) * 0.5
    a_norm_w = 1.0 + 0.02 * jax.random.normal(ks[3], (N_LAYERS_A, A_DV), f32)
    a_w_out = jax.random.normal(ks[4], (N_LAYERS_A, A_VAL_WIDTH, D), f32) * (A_VAL_WIDTH ** -0.5) * DEEPNORM_BETA
    b_w_in = jax.random.normal(ks[5], (N_LAYERS_B, D, B_IN_WIDTH), f32) * (D ** -0.5)
    b_conv_w = jax.random.normal(ks[6], (N_LAYERS_B, B_CONV_WIDTH, B_CONV_CH), f32) * (B_CONV_WIDTH ** -0.5)
    b_a_log = jnp.log(jax.random.uniform(ks[7], (N_LAYERS_B, B_HEADS), f32, 1.0, 16.0))
    dt = jnp.exp(jax.random.uniform(ks[8], (N_LAYERS_B, B_HEADS), f32, math.log(1e-3), math.log(1e-1)))
    b_dt_bias = dt + jnp.log(-jnp.expm1(-dt))
    b_norm_w = 1.0 + 0.02 * jax.random.normal(ks[9], (N_LAYERS_B, B_DV), f32)
    b_w_out = jax.random.normal(ks[10], (N_LAYERS_B, B_VAL_WIDTH, D), f32) * (B_VAL_WIDTH ** -0.5) * DEEPNORM_BETA
    ln_g = 1.0 + 0.02 * jax.random.normal(ks[11], (DEPTH, D), f32)
    ln_b = 0.02 * jax.random.normal(ks[12], (DEPTH, D), f32)
    return {"x": x, "a_w_in": a_w_in, "a_lower_bounds": a_lower_bounds, "a_norm_w": a_norm_w,
            "a_w_out": a_w_out, "b_w_in": b_w_in, "b_conv_w": b_conv_w, "b_a_log": b_a_log,
            "b_dt_bias": b_dt_bias, "b_norm_w": b_norm_w, "b_w_out": b_w_out,
            "ln_g": ln_g, "ln_b": ln_b}


def reference(x, a_w_in, a_lower_bounds, a_norm_w, a_w_out, b_w_in, b_conv_w, b_a_log,
              b_dt_bias, b_norm_w, b_w_out, ln_g, ln_b):
    lb_all = jnp.cumsum(jax.nn.softmax(a_lower_bounds.astype(jnp.float32), axis=0), axis=0)
    for i in range(DEPTH):
        j = i // N_MIXERS
        if i % N_MIXERS == 0:
            y = hgrn2_mixer(x, a_w_in[j], lb_all[i], a_norm_w[j], a_w_out[j])
        else:
            y = gated_deltanet_mixer(x, b_w_in[j], b_conv_w[j], b_a_log[j], b_dt_bias[j],
                                     b_norm_w[j], b_w_out[j])
        x = layer_norm(DEEPNORM_ALPHA * x + y, ln_g[i], ln_b[i])
    return x
```

```python
import functools

import jax
import jax.numpy as jnp
from jax import lax
from jax.experimental import pallas as pl
from jax.experimental.pallas import tpu as pltpu

F32 = jnp.float32
BF16 = jnp.bfloat16

HEAD_DIM = 128
CHUNK = 64
SUB = 16
GROUP = 128
CONV_TAPS = 4
SUBLANES = 8
LN_EPS = 1e-5
RMS_EPS = 1e-6
L2_EPS = 1e-6
VMEM_LIMIT = 56 * 1024 * 1024


def _dot(a, b):
    return jnp.dot(a.astype(BF16), b.astype(BF16), preferred_element_type=F32)


def _dot_nt(a, b):
    return lax.dot_general(a.astype(BF16), b.astype(BF16), (((1,), (1,)), ((), ())),
                           preferred_element_type=F32)


def _dot_tn(a, b):
    return lax.dot_general(a.astype(BF16), b.astype(BF16), (((0,), (0,)), ((), ())),
                           preferred_element_type=F32)


def _silu(x):
    return x * jax.nn.sigmoid(x)


def _gated_rmsnorm(o, z, w):
    ms = jnp.mean(o * o, axis=-1, keepdims=True)
    return o * lax.rsqrt(ms + RMS_EPS) * w * _silu(z)


def _matmul_kernel(x_ref, w_ref, o_ref):
    o_ref[...] = jnp.dot(x_ref[...], w_ref[...], preferred_element_type=F32)


def _matmul(x, w, tm, tn):
    m, k = x.shape
    n = w.shape[1]
    tm, tn = min(tm, m), min(tn, n)
    return pl.pallas_call(
        _matmul_kernel,
        grid=(m // tm, n // tn),
        in_specs=[pl.BlockSpec((tm, k), lambda i, j: (i, 0)),
                  pl.BlockSpec((k, tn), lambda i, j: (0, j))],
        out_specs=pl.BlockSpec((tm, tn), lambda i, j: (i, j)),
        out_shape=jax.ShapeDtypeStruct((m, n), F32),
        compiler_params=pltpu.CompilerParams(
            dimension_semantics=("parallel", "parallel"), vmem_limit_bytes=VMEM_LIMIT),
        name="in_proj",
    )(x, w)


def _split_bf16(v):
    hi = v.astype(BF16)
    lo = (v - hi.astype(F32)).astype(BF16)
    return hi, lo


def _gate_proj_kernel(x_ref, w_ref, o_ref):
    xh, xl = _split_bf16(x_ref[...])
    wh, wl = _split_bf16(w_ref[...])
    dot = functools.partial(jnp.dot, preferred_element_type=F32)
    o_ref[...] = dot(xh, wh) + (dot(xl, wh) + dot(xh, wl))


def _gate_proj(x, w, tm):
    m, k = x.shape
    n = w.shape[1]
    tm = min(tm, m)
    return pl.pallas_call(
        _gate_proj_kernel,
        grid=(m // tm,),
        in_specs=[pl.BlockSpec((tm, k), lambda i: (i, 0)),
                  pl.BlockSpec((k, n), lambda i: (0, 0))],
        out_specs=pl.BlockSpec((tm, n), lambda i: (i, 0)),
        out_shape=jax.ShapeDtypeStruct((m, n), F32),
        compiler_params=pltpu.CompilerParams(
            dimension_semantics=("parallel",), vmem_limit_bytes=VMEM_LIMIT),
        name="gate_proj",
    )(x, w)


def _out_proj_ln_kernel(y_ref, w_ref, x_ref, g_ref, b_ref, o_ref, obf_ref, acc_ref, *, alpha, nk):
    k = pl.program_id(1)

    @pl.when(k == 0)
    def _():
        acc_ref[...] = jnp.zeros_like(acc_ref)

    acc_ref[...] += jnp.dot(y_ref[...], w_ref[...], preferred_element_type=F32)

    @pl.when(k == nk - 1)
    def _():
        h = alpha * x_ref[...] + acc_ref[...]
        mu = jnp.mean(h, axis=-1, keepdims=True)
        c = h - mu
        var = jnp.mean(c * c, axis=-1, keepdims=True)
        out = c * lax.rsqrt(var + LN_EPS) * g_ref[...] + b_ref[...]
        o_ref[...] = out
        obf_ref[...] = out.astype(BF16)


def _out_proj_ln(y, w, x, g, b, alpha, tm, tk):
    m, k = y.shape
    n = w.shape[1]
    tm, tk = min(tm, m), min(tk, k)
    nk = k // tk
    return pl.pallas_call(
        functools.partial(_out_proj_ln_kernel, alpha=alpha, nk=nk),
        grid=(m // tm, nk),
        in_specs=[pl.BlockSpec((tm, tk), lambda i, j: (i, j)),
                  pl.BlockSpec((tk, n), lambda i, j: (j, 0)),
                  pl.BlockSpec((tm, n), lambda i, j: (i, 0)),
                  pl.BlockSpec((1, n), lambda i, j: (0, 0)),
                  pl.BlockSpec((1, n), lambda i, j: (0, 0))],
        out_specs=[pl.BlockSpec((tm, n), lambda i, j: (i, 0)),
                   pl.BlockSpec((tm, n), lambda i, j: (i, 0))],
        out_shape=[jax.ShapeDtypeStruct((m, n), F32), jax.ShapeDtypeStruct((m, n), BF16)],
        scratch_shapes=[pltpu.VMEM((tm, n), F32)],
        compiler_params=pltpu.CompilerParams(
            dimension_semantics=("parallel", "arbitrary"), vmem_limit_bytes=VMEM_LIMIT),
        name="out_proj_ln",
    )(y, w, x, g, b)


def _hgrn2_intra(q, k, cum):
    lane = lax.broadcasted_iota(jnp.int32, (SUB, CHUNK), 1)
    row = lax.broadcasted_iota(jnp.int32, (SUB, 1), 0)
    blocks = []
    for blk in range(CHUNK // SUB):
        r0 = blk * SUB
        q_b = q[r0:r0 + SUB]
        cum_b = cum[r0:r0 + SUB]
        a_b = jnp.zeros((SUB, CHUNK), F32)
        if blk > 0:
            start = cum[r0 - 1:r0]
            q_s = q_b * jnp.exp(cum_b - start)
            k_s = k * jnp.exp(jnp.minimum(start - cum, 0.0))
            a_b = jnp.where(lane < r0, _dot_nt(q_s, k_s), 0.0)
        for j in range(SUB):
            s = r0 + j
            e = jnp.exp(jnp.minimum(cum_b - cum[s:s + 1], 0.0))
            col = jnp.sum(q_b * (k[s:s + 1] * e), axis=-1, keepdims=True)
            a_b = jnp.where(lane == s, jnp.where(row >= j, col, 0.0), a_b)
        blocks.append(a_b)
    return jnp.concatenate(blocks, axis=0)


def _hgrn2_kernel(q_ref, f_ref, i_ref, z_ref, lbp_ref, nw_ref, tri_ref, y_ref, st_ref,
                  *, hb, t_blk, layer):
    @pl.when(pl.program_id(2) == 0)
    def _():
        st_ref[...] = jnp.zeros_like(st_ref)

    tri = tri_ref[...]
    nw = nw_ref[...]
    for h in range(hb):
        cols = slice(h * HEAD_DIM, (h + 1) * HEAD_DIM)
        lbp = lbp_ref[:, cols]
        ex = jnp.exp(lbp - jnp.max(lbp, axis=0, keepdims=True))
        lb = jnp.sum(ex[:layer + 1], axis=0, keepdims=True) / jnp.sum(ex, axis=0, keepdims=True)

        q = _silu(q_ref[:, cols])
        forget = lb + (1.0 - lb) * jax.nn.sigmoid(f_ref[:, cols])
        k = 1.0 - forget
        log_f = jnp.log(forget)
        v = i_ref[:, cols]

        g_hi, g_lo = _split_bf16(log_f)
        cs = jnp.dot(tri, jnp.concatenate([g_hi, g_lo], axis=1), preferred_element_type=F32)
        cum_all = cs[:, :HEAD_DIM] + cs[:, HEAD_DIM:]

        st = st_ref[h]
        outs = []
        for c in range(t_blk // CHUNK):
            rows = slice(c * CHUNK, (c + 1) * CHUNK)
            q_c, k_c, v_c, cum = q[rows], k[rows], v[rows], cum_all[rows]
            last = cum[CHUNK - 1:CHUNK]
            a = _hgrn2_intra(q_c, k_c, cum)
            o = _dot(a, v_c) + _dot_nt(q_c * jnp.exp(cum), st)
            st = jnp.exp(last) * st + _dot_tn(v_c, k_c * jnp.exp(last - cum))
            outs.append(o)
        st_ref[h] = st
        o_all = jnp.concatenate(outs, axis=0)
        y_ref[:, cols] = _gated_rmsnorm(o_all, z_ref[:, cols], nw).astype(y_ref.dtype)


def _block_diag_tri(t_blk):
    r = lax.broadcasted_iota(jnp.int32, (t_blk, t_blk), 0)
    c = lax.broadcasted_iota(jnp.int32, (t_blk, t_blk), 1)
    return ((c <= r) & (c // CHUNK == r // CHUNK)).astype(BF16)


def _hgrn2_scan(hproj, lb_params, norm_w, batch, seq, d, layer, hb, t_blk):
    hb = min(hb, d // HEAD_DIM)
    t_blk = min(t_blk, seq)
    w = hb * HEAD_DIM
    nhb = d // w
    nt = seq // t_blk

    def col_spec(part):
        return pl.BlockSpec((t_blk, w), lambda b, h, t: (b * nt + t, part * nhb + h))

    return pl.pallas_call(
        functools.partial(_hgrn2_kernel, hb=hb, t_blk=t_blk, layer=layer),
        grid=(batch, nhb, nt),
        in_specs=[col_spec(0), col_spec(1), col_spec(2), col_spec(3),
                  pl.BlockSpec((lb_params.shape[0], w), lambda b, h, t: (0, h)),
                  pl.BlockSpec((1, HEAD_DIM), lambda b, h, t: (0, 0)),
                  pl.BlockSpec((t_blk, t_blk), lambda b, h, t: (0, 0))],
        out_specs=pl.BlockSpec((t_blk, w), lambda b, h, t: (b * nt + t, h)),
        out_shape=jax.ShapeDtypeStruct((batch * seq, d), BF16),
        scratch_shapes=[pltpu.VMEM((hb, HEAD_DIM, HEAD_DIM), F32)],
        compiler_params=pltpu.CompilerParams(
            dimension_semantics=("parallel", "parallel", "arbitrary"),
            vmem_limit_bytes=VMEM_LIMIT),
        name="hgrn2_scan",
    )(hproj, hproj, hproj, hproj, lb_params, norm_w, _block_diag_tri(t_blk))


def _causal_conv_silu(x, tail, w):
    row8 = lax.broadcasted_iota(jnp.int32, (SUBLANES, 1), 0)
    acc = x * w[CONV_TAPS - 1:CONV_TAPS]
    for j in range(1, CONV_TAPS):
        shifted = pltpu.roll(x, j, 0)
        head = jnp.where(row8 < j, pltpu.roll(tail, j, 0), shifted[:SUBLANES])
        shifted = jnp.concatenate([head, shifted[SUBLANES:]], axis=0)
        acc = acc + shifted * w[CONV_TAPS - 1 - j:CONV_TAPS - j]
    return _silu(acc)


def _l2_normalize(x):
    return x * lax.rsqrt(jnp.sum(x * x, axis=-1, keepdims=True) + L2_EPS)


def _gdn_kernel(alog_ref, dtb_ref, q_ref, k_ref, v_ref, z_ref, cwq_ref, cwk_ref, cwv_ref,
                ab_ref, nw_ref, y_ref, s_ref, tail_ref, *, hb, t_blk, heads):
    @pl.when(pl.program_id(2) == 0)
    def _():
        s_ref[...] = jnp.zeros_like(s_ref)
        tail_ref[...] = jnp.zeros_like(tail_ref)

    nw = nw_ref[...]
    r = lax.broadcasted_iota(jnp.int32, (GROUP, GROUP), 0)
    c = lax.broadcasted_iota(jnp.int32, (GROUP, GROUP), 1)
    same_chunk = (r // CHUNK) == (c // CHUNK)
    eye = r == c
    incl = same_chunk & (c <= r)
    strict = same_chunk & (c < r)
    chunks_per_group = GROUP // CHUNK

    for h in range(hb):
        cols = slice(h * HEAD_DIM, (h + 1) * HEAD_DIM)
        head = pl.program_id(1) * hb + h

        streams = []
        for idx, (x_ref, cw_ref) in enumerate(((q_ref, cwq_ref), (k_ref, cwk_ref), (v_ref, cwv_ref))):
            x = x_ref[:, cols]
            streams.append(_causal_conv_silu(x, tail_ref[idx, h], cw_ref[:, cols]))
            tail_ref[idx, h] = x[t_blk - SUBLANES:]
        q_all = _l2_normalize(streams[0]) * (HEAD_DIM ** -0.5)
        k_all = _l2_normalize(streams[1])
        v_all = streams[2]

        neg_a = -jnp.exp(jnp.full((1, GROUP), alog_ref[head], F32))
        dt_bias = jnp.full((1, GROUP), dtb_ref[head], F32)

        state = s_ref[h]
        outs = []
        for grp in range(t_blk // GROUP):
            rows = slice(grp * GROUP, (grp + 1) * GROUP)
            q_g, k_g, v_g = q_all[rows], k_all[rows], v_all[rows]
            beta_row = jax.nn.sigmoid(ab_ref[0, grp, pl.ds(head, 1), :])
            a_row = ab_ref[0, grp, pl.ds(heads + head, 1), :] + dt_bias
            g_row = neg_a * (jnp.maximum(a_row, 0.0) + jnp.log1p(jnp.exp(-jnp.abs(a_row))))

            cum_col = jnp.sum(jnp.where(incl, g_row, 0.0), axis=1, keepdims=True)
            cum_row = jnp.sum(jnp.where(eye, cum_col, 0.0), axis=0, keepdims=True)
            beta_col = jnp.sum(jnp.where(eye, beta_row, 0.0), axis=1, keepdims=True)

            decay = jnp.exp(jnp.minimum(cum_col - cum_row, 0.0))
            kk = _dot_nt(k_g, k_g)
            qk = _dot_nt(q_g, k_g) * jnp.where(incl, decay, 0.0)

            neg_l = -(beta_col * kk) * jnp.where(strict, decay, 0.0)
            inv = jnp.where(eye, 1.0, 0.0) + neg_l
            power = neg_l
            for _ in range(5):
                power = _dot(power, power)
                inv = inv + _dot(inv, power)

            e_cum = jnp.exp(cum_col)
            rhs = jnp.concatenate([beta_col * v_g, (beta_col * e_cum) * k_g], axis=1)
            sol = _dot(inv, rhs)
            u0, w = sol[:, :HEAD_DIM], sol[:, HEAD_DIM:]
            q_dec = q_g * e_cum

            u_parts = []
            for ci in range(chunks_per_group):
                cr = slice(ci * CHUNK, (ci + 1) * CHUNK)
                last = cum_col[(ci + 1) * CHUNK - 1:(ci + 1) * CHUNK]
                u = u0[cr] - _dot(w[cr], state)
                u_parts.append(u)
                rest = [u0[(ci + 1) * CHUNK:]] if ci + 1 < chunks_per_group else []
                u_full = jnp.concatenate(u_parts + rest, axis=0)
                outs.append(_dot(q_dec[cr], state) + _dot(qk[cr], u_full))
                k_dec = k_g[cr] * jnp.exp(last - cum_col[cr])
                state = jnp.exp(last) * state + _dot_tn(k_dec, u)
        s_ref[h] = state
        o_all = jnp.concatenate(outs, axis=0)
        y_ref[:, cols] = _gated_rmsnorm(o_all, z_ref[:, cols], nw).astype(y_ref.dtype)


def _gdn_scan(hproj, conv_w, ab_rows, a_log, dt_bias, norm_w, batch, seq, d, hb, t_blk):
    heads = d // HEAD_DIM
    hb = min(hb, heads)
    t_blk = min(t_blk, seq)
    w = hb * HEAD_DIM
    nhb = d // w
    nt = seq // t_blk
    gpb = t_blk // GROUP

    def col_spec(part):
        return pl.BlockSpec((t_blk, w), lambda b, h, t: (b * nt + t, part * nhb + h))

    def conv_spec(part):
        return pl.BlockSpec((CONV_TAPS, w), lambda b, h, t: (0, part * nhb + h))

    smem = pl.BlockSpec(memory_space=pltpu.SMEM)
    return pl.pallas_call(
        functools.partial(_gdn_kernel, hb=hb, t_blk=t_blk, heads=heads),
        grid=(batch, nhb, nt),
        in_specs=[smem, smem, col_spec(0), col_spec(1), col_spec(2), col_spec(3),
                  conv_spec(0), conv_spec(1), conv_spec(2),
                  pl.BlockSpec((1, gpb, 2 * heads, GROUP), lambda b, h, t: (b, t, 0, 0)),
                  pl.BlockSpec((1, HEAD_DIM), lambda b, h, t: (0, 0))],
        out_specs=pl.BlockSpec((t_blk, w), lambda b, h, t: (b * nt + t, h)),
        out_shape=jax.ShapeDtypeStruct((batch * seq, d), BF16),
        scratch_shapes=[pltpu.VMEM((hb, HEAD_DIM, HEAD_DIM), F32),
                        pltpu.VMEM((3, hb, SUBLANES, HEAD_DIM), F32)],
        compiler_params=pltpu.CompilerParams(
            dimension_semantics=("parallel", "parallel", "arbitrary"),
            vmem_limit_bytes=VMEM_LIMIT),
        name="gdn_scan",
    )(a_log, dt_bias, hproj, hproj, hproj, hproj, conv_w, conv_w, conv_w, ab_rows, norm_w)


def kernel(x, a_w_in, a_lower_bounds, a_norm_w, a_w_out, b_w_in, b_conv_w, b_a_log, b_dt_bias,
           b_norm_w, b_w_out, ln_g, ln_b):
    batch, seq, d = x.shape
    heads = d // HEAD_DIM
    depth = ln_g.shape[0]
    assert depth == 2 and a_w_in.shape[0] == 1 and b_w_in.shape[0] == 1
    assert a_w_in.shape[2] == 4 * d and b_w_in.shape[2] == 4 * d + 2 * heads
    assert seq % GROUP == 0 and d % HEAD_DIM == 0
    alpha = (2.0 * depth) ** 0.25
    m = batch * seq
    x2 = x.reshape(m, d)

    hproj = _matmul(x2.astype(BF16), a_w_in[0].astype(BF16), tm=1024, tn=512)
    y = _hgrn2_scan(hproj, a_lower_bounds, a_norm_w, batch, seq, d, layer=0, hb=2, t_blk=256)
    x2, x2_bf = _out_proj_ln(y, a_w_out[0].astype(BF16), x2, ln_g[0:1], ln_b[0:1], alpha,
                             tm=256, tk=512)

    w_in = b_w_in[0]
    hproj = _matmul(x2_bf, w_in[:, :4 * d].astype(BF16), tm=1024, tn=512)
    w_gate = jnp.pad(w_in[:, 4 * d:], ((0, 0), (0, HEAD_DIM - 2 * heads)))
    ab = _gate_proj(x2, w_gate, tm=512)[:, :2 * heads]
    ab_rows = ab.reshape(batch, seq // GROUP, GROUP, 2 * heads).transpose(0, 1, 3, 2)
    y = _gdn_scan(hproj, b_conv_w[0], ab_rows, b_a_log[0], b_dt_bias[0], b_norm_w,
                  batch, seq, d, hb=2, t_blk=256)
    x2, _ = _out_proj_ln(y, b_w_out[0].astype(BF16), x2, ln_g[1:2], ln_b[1:2], alpha,
                         tm=256, tk=512)
    return x2.reshape(batch, seq, d)
```

```python
import functools

import jax
import jax.numpy as jnp
from jax import lax
from jax.experimental import pallas as pl
from jax.experimental.pallas import tpu as pltpu

F32 = jnp.float32
BF16 = jnp.bfloat16

HEAD_DIM = 128
CHUNK = 64
SUB = 16
GROUP = 128
CONV_TAPS = 4
SUBLANES = 8
LN_EPS = 1e-5
RMS_EPS = 1e-6
L2_EPS = 1e-6
VMEM_LIMIT = 56 * 1024 * 1024


def _dot(a, b):
    return jnp.dot(a.astype(BF16), b.astype(BF16), preferred_element_type=F32)


def _dot_nt(a, b):
    return lax.dot_general(a.astype(BF16), b.astype(BF16), (((1,), (1,)), ((), ())),
                           preferred_element_type=F32)


def _dot_tn(a, b):
    return lax.dot_general(a.astype(BF16), b.astype(BF16), (((0,), (0,)), ((), ())),
                           preferred_element_type=F32)


def _silu(x):
    return x * jax.nn.sigmoid(x)


def _gated_rmsnorm(o, z, w):
    ms = jnp.mean(o * o, axis=-1, keepdims=True)
    return o * lax.rsqrt(ms + RMS_EPS) * w * _silu(z)


def _matmul_kernel(x_ref, w_ref, o_ref):
    o_ref[...] = jnp.dot(x_ref[...], w_ref[...], preferred_element_type=F32)


def _matmul(x, w, tm, tn):
    m, k = x.shape
    n = w.shape[1]
    tm, tn = min(tm, m), min(tn, n)
    return pl.pallas_call(
        _matmul_kernel,
        grid=(m // tm, n // tn),
        in_specs=[pl.BlockSpec((tm, k), lambda i, j: (i, 0)),
                  pl.BlockSpec((k, tn), lambda i, j: (0, j))],
        out_specs=pl.BlockSpec((tm, tn), lambda i, j: (i, j)),
        out_shape=jax.ShapeDtypeStruct((m, n), F32),
        compiler_params=pltpu.CompilerParams(
            dimension_semantics=("parallel", "parallel"), vmem_limit_bytes=VMEM_LIMIT),
        name="in_proj",
    )(x, w)


def _split_bf16(v):
    hi = v.astype(BF16)
    lo = (v - hi.astype(F32)).astype(BF16)
    return hi, lo


def _gate_proj_kernel(x_ref, w_ref, o_ref):
    xh, xl = _split_bf16(x_ref[...])
    wh, wl = _split_bf16(w_ref[...])
    dot = functools.partial(jnp.dot, preferred_element_type=F32)
    o_ref[...] = dot(xh, wh) + (dot(xl, wh) + dot(xh, wl))


def _gate_proj(x, w, tm):
    m, k = x.shape
    n = w.shape[1]
    tm = min(tm, m)
    return pl.pallas_call(
        _gate_proj_kernel,
        grid=(m // tm,),
        in_specs=[pl.BlockSpec((tm, k), lambda i: (i, 0)),
                  pl.BlockSpec((k, n), lambda i: (0, 0))],
        out_specs=pl.BlockSpec((tm, n), lambda i: (i, 0)),
        out_shape=jax.ShapeDtypeStruct((m, n), F32),
        compiler_params=pltpu.CompilerParams(
            dimension_semantics=("parallel",), vmem_limit_bytes=VMEM_LIMIT),
        name="gate_proj",
    )(x, w)


def _out_proj_ln_kernel(y_ref, w_ref, x_ref, g_ref, b_ref, o_ref, obf_ref, acc_ref, *, alpha, nk):
    k = pl.program_id(1)

    @pl.when(k == 0)
    def _():
        acc_ref[...] = jnp.zeros_like(acc_ref)

    acc_ref[...] += jnp.dot(y_ref[...], w_ref[...], preferred_element_type=F32)

    @pl.when(k == nk - 1)
    def _():
        h = alpha * x_ref[...] + acc_ref[...]
        mu = jnp.mean(h, axis=-1, keepdims=True)
        c = h - mu
        var = jnp.mean(c * c, axis=-1, keepdims=True)
        out = c * lax.rsqrt(var + LN_EPS) * g_ref[...] + b_ref[...]
        o_ref[...] = out
        obf_ref[...] = out.astype(BF16)


def _out_proj_ln(y, w, x, g, b, alpha, tm, tk):
    m, k = y.shape
    n = w.shape[1]
    tm, tk = min(tm, m), min(tk, k)
    nk = k // tk
    return pl.pallas_call(
        functools.partial(_out_proj_ln_kernel, alpha=alpha, nk=nk),
        grid=(m // tm, nk),
        in_specs=[pl.BlockSpec((tm, tk), lambda i, j: (i, j)),
                  pl.BlockSpec((tk, n), lambda i, j: (j, 0)),
                  pl.BlockSpec((tm, n), lambda i, j: (i, 0)),
                  pl.BlockSpec((1, n), lambda i, j: (0, 0)),
                  pl.BlockSpec((1, n), lambda i, j: (0, 0))],
        out_specs=[pl.BlockSpec((tm, n), lambda i, j: (i, 0)),
                   pl.BlockSpec((tm, n), lambda i, j: (i, 0))],
        out_shape=[jax.ShapeDtypeStruct((m, n), F32), jax.ShapeDtypeStruct((m, n), BF16)],
        scratch_shapes=[pltpu.VMEM((tm, n), F32)],
        compiler_params=pltpu.CompilerParams(
            dimension_semantics=("parallel", "arbitrary"), vmem_limit_bytes=VMEM_LIMIT),
        name="out_proj_ln",
    )(y, w, x, g, b)


def _hgrn2_intra(q, k, cum):
    lane = lax.broadcasted_iota(jnp.int32, (SUB, CHUNK), 1)
    row = lax.broadcasted_iota(jnp.int32, (SUB, 1), 0)
    blocks = []
    for blk in range(CHUNK // SUB):
        r0 = blk * SUB
        q_b = q[r0:r0 + SUB]
        cum_b = cum[r0:r0 + SUB]
        a_b = jnp.zeros((SUB, CHUNK), F32)
        if blk > 0:
            start = cum[r0 - 1:r0]
            q_s = q_b * jnp.exp(cum_b - start)
            k_s = k * jnp.exp(jnp.minimum(start - cum, 0.0))
            a_b = jnp.where(lane < r0, _dot_nt(q_s, k_s), 0.0)
        for j in range(SUB):
            s = r0 + j
            e = jnp.exp(jnp.minimum(cum_b - cum[s:s + 1], 0.0))
            col = jnp.sum(q_b * (k[s:s + 1] * e), axis=-1, keepdims=True)
            a_b = jnp.where(lane == s, jnp.where(row >= j, col, 0.0), a_b)
        blocks.append(a_b)
    return jnp.concatenate(blocks, axis=0)


def _hgrn2_kernel(q_ref, f_ref, i_ref, z_ref, lbp_ref, nw_ref, tri_ref, y_ref, st_ref,
                  *, hb, t_blk, layer):
    @pl.when(pl.program_id(2) == 0)
    def _():
        st_ref[...] = jnp.zeros_like(st_ref)

    tri = tri_ref[...]
    nw = nw_ref[...]
    for h in range(hb):
        cols = slice(h * HEAD_DIM, (h + 1) * HEAD_DIM)
        lbp = lbp_ref[:, cols]
        ex = jnp.exp(lbp - jnp.max(lbp, axis=0, keepdims=True))
        lb = jnp.sum(ex[:layer + 1], axis=0, keepdims=True) / jnp.sum(ex, axis=0, keepdims=True)

        q = _silu(q_ref[:, cols])
        forget = lb + (1.0 - lb) * jax.nn.sigmoid(f_ref[:, cols])
        k = 1.0 - forget
        log_f = jnp.log(forget)
        v = i_ref[:, cols]

        g_hi, g_lo = _split_bf16(log_f)
        cs = jnp.dot(tri, jnp.concatenate([g_hi, g_lo], axis=1), preferred_element_type=F32)
        cum_all = cs[:, :HEAD_DIM] + cs[:, HEAD_DIM:]

        st = st_ref[h]
        outs = []
        for c in range(t_blk // CHUNK):
            rows = slice(c * CHUNK, (c + 1) * CHUNK)
            q_c, k_c, v_c, cum = q[rows], k[rows], v[rows], cum_all[rows]
            last = cum[CHUNK - 1:CHUNK]
            a = _hgrn2_intra(q_c, k_c, cum)
            o = _dot(a, v_c) + _dot_nt(q_c * jnp.exp(cum), st)
            st = jnp.exp(last) * st + _dot_tn(v_c, k_c * jnp.exp(last - cum))
            outs.append(o)
        st_ref[h] = st
        o_all = jnp.concatenate(outs, axis=0)
        y_ref[:, cols] = _gated_rmsnorm(o_all, z_ref[:, cols], nw).astype(y_ref.dtype)


def _block_diag_tri(t_blk):
    r = lax.broadcasted_iota(jnp.int32, (t_blk, t_blk), 0)
    c = lax.broadcasted_iota(jnp.int32, (t_blk, t_blk), 1)
    return ((c <= r) & (c // CHUNK == r // CHUNK)).astype(BF16)


def _hgrn2_scan(hproj, lb_params, norm_w, batch, seq, d, layer, hb, t_blk):
    hb = min(hb, d // HEAD_DIM)
    t_blk = min(t_blk, seq)
    w = hb * HEAD_DIM
    nhb = d // w
    nt = seq // t_blk

    def col_spec(part):
        return pl.BlockSpec((t_blk, w), lambda b, h, t: (b * nt + t, part * nhb + h))

    return pl.pallas_call(
        functools.partial(_hgrn2_kernel, hb=hb, t_blk=t_blk, layer=layer),
        grid=(batch, nhb, nt),
        in_specs=[col_spec(0), col_spec(1), col_spec(2), col_spec(3),
                  pl.BlockSpec((lb_params.shape[0], w), lambda b, h, t: (0, h)),
                  pl.BlockSpec((1, HEAD_DIM), lambda b, h, t: (0, 0)),
                  pl.BlockSpec((t_blk, t_blk), lambda b, h, t: (0, 0))],
        out_specs=pl.BlockSpec((t_blk, w), lambda b, h, t: (b * nt + t, h)),
        out_shape=jax.ShapeDtypeStruct((batch * seq, d), BF16),
        scratch_shapes=[pltpu.VMEM((hb, HEAD_DIM, HEAD_DIM), F32)],
        compiler_params=pltpu.CompilerParams(
            dimension_semantics=("parallel", "parallel", "arbitrary"),
            vmem_limit_bytes=VMEM_LIMIT),
        name="hgrn2_scan",
    )(hproj, hproj, hproj, hproj, lb_params, norm_w, _block_diag_tri(t_blk))


def _causal_conv_silu(x, tail, w):
    row8 = lax.broadcasted_iota(jnp.int32, (SUBLANES, 1), 0)
    acc = x * w[CONV_TAPS - 1:CONV_TAPS]
    for j in range(1, CONV_TAPS):
        shifted = pltpu.roll(x, j, 0)
        head = jnp.where(row8 < j, pltpu.roll(tail, j, 0), shifted[:SUBLANES])
        shifted = jnp.concatenate([head, shifted[SUBLANES:]], axis=0)
        acc = acc + shifted * w[CONV_TAPS - 1 - j:CONV_TAPS - j]
    return _silu(acc)


def _l2_normalize(x):
    return x * lax.rsqrt(jnp.sum(x * x, axis=-1, keepdims=True) + L2_EPS)


def _gdn_kernel(alog_ref, dtb_ref, q_ref, k_ref, v_ref, z_ref, cwq_ref, cwk_ref, cwv_ref,
                ab_ref, nw_ref, y_ref, s_ref, tail_ref, *, hb, t_blk, heads):
    @pl.when(pl.program_id(2) == 0)
    def _():
        s_ref[...] = jnp.zeros_like(s_ref)
        tail_ref[...] = jnp.zeros_like(tail_ref)

    nw = nw_ref[...]
    r = lax.broadcasted_iota(jnp.int32, (GROUP, GROUP), 0)
    c = lax.broadcasted_iota(jnp.int32, (GROUP, GROUP), 1)
    same_chunk = (r // CHUNK) == (c // CHUNK)
    eye = r == c
    incl = same_chunk & (c <= r)
    strict = same_chunk & (c < r)
    chunks_per_group = GROUP // CHUNK
    n_groups = t_blk // GROUP
    hrange = range(hb)
    problems = [(h, grp) for h in hrange for grp in range(n_groups)]

    q_all, k_all, v_all = [], [], []
    for h in hrange:
        cols = slice(h * HEAD_DIM, (h + 1) * HEAD_DIM)
        streams = []
        for idx, (x_ref, cw_ref) in enumerate(((q_ref, cwq_ref), (k_ref, cwk_ref), (v_ref, cwv_ref))):
            x = x_ref[:, cols]
            streams.append(_causal_conv_silu(x, tail_ref[idx, h], cw_ref[:, cols]))
            tail_ref[idx, h] = x[t_blk - SUBLANES:]
        q_all.append(_l2_normalize(streams[0]) * (HEAD_DIM ** -0.5))
        k_all.append(_l2_normalize(streams[1]))
        v_all.append(streams[2])

    def rows_of(grp):
        return slice(grp * GROUP, (grp + 1) * GROUP)

    kk = {p: _dot_nt(k_all[p[0]][rows_of(p[1])], k_all[p[0]][rows_of(p[1])]) for p in problems}
    qk = {p: _dot_nt(q_all[p[0]][rows_of(p[1])], k_all[p[0]][rows_of(p[1])]) for p in problems}

    cum_col, e_cum, rhs, inv, power = {}, {}, {}, {}, {}
    for p in problems:
        h, grp = p
        head = pl.program_id(1) * hb + h
        neg_a = -jnp.exp(jnp.full((1, GROUP), alog_ref[head], F32))
        beta_row = jax.nn.sigmoid(ab_ref[0, grp, pl.ds(head, 1), :])
        a_row = ab_ref[0, grp, pl.ds(heads + head, 1), :] + jnp.full((1, GROUP), dtb_ref[head], F32)
        g_row = neg_a * (jnp.maximum(a_row, 0.0) + jnp.log1p(jnp.exp(-jnp.abs(a_row))))

        cum_col[p] = jnp.sum(jnp.where(incl, g_row, 0.0), axis=1, keepdims=True)
        cum_row = jnp.sum(jnp.where(eye, cum_col[p], 0.0), axis=0, keepdims=True)
        beta_col = jnp.sum(jnp.where(eye, beta_row, 0.0), axis=1, keepdims=True)

        decay = jnp.exp(jnp.minimum(cum_col[p] - cum_row, 0.0))
        qk[p] = qk[p] * jnp.where(incl, decay, 0.0)
        power[p] = -(beta_col * kk[p]) * jnp.where(strict, decay, 0.0)
        inv[p] = jnp.where(eye, 1.0, 0.0) + power[p]
        e_cum[p] = jnp.exp(cum_col[p])
        rhs[p] = jnp.concatenate([beta_col * v_all[h][rows_of(grp)],
                                  (beta_col * e_cum[p]) * k_all[h][rows_of(grp)]], axis=1)

    for _ in range(5):
        for p in problems:
            power[p] = _dot(power[p], power[p])
        for p in problems:
            inv[p] = inv[p] + _dot(inv[p], power[p])
    sol = {p: _dot(inv[p], rhs[p]) for p in problems}

    state = [s_ref[h] for h in hrange]
    outs = [[] for _ in hrange]
    for grp in range(n_groups):
        u_parts = [[] for _ in hrange]
        for ci in range(chunks_per_group):
            cr = slice(ci * CHUNK, (ci + 1) * CHUNK)
            last_row = (ci + 1) * CHUNK - 1
            ws = []
            for h in hrange:
                p = (h, grp)
                q_dec = q_all[h][rows_of(grp)][cr] * e_cum[p][cr]
                ws.append(_dot(jnp.concatenate([sol[p][cr, HEAD_DIM:], q_dec], axis=0), state[h]))
            u = [sol[(h, grp)][cr, :HEAD_DIM] - ws[h][:CHUNK] for h in hrange]
            for h in hrange:
                p = (h, grp)
                last = cum_col[p][last_row:last_row + 1]
                k_dec = k_all[h][rows_of(grp)][cr] * jnp.exp(last - cum_col[p][cr])
                state[h] = jnp.exp(last) * state[h] + _dot_tn(k_dec, u[h])
            for h in hrange:
                p = (h, grp)
                u_parts[h].append(u[h])
                rest = [sol[p][(ci + 1) * CHUNK:, :HEAD_DIM]] if ci + 1 < chunks_per_group else []
                u_full = jnp.concatenate(u_parts[h] + rest, axis=0)
                outs[h].append(ws[h][CHUNK:] + _dot(qk[p][cr], u_full))
    for h in hrange:
        cols = slice(h * HEAD_DIM, (h + 1) * HEAD_DIM)
        s_ref[h] = state[h]
        o_all = jnp.concatenate(outs[h], axis=0)
        y_ref[:, cols] = _gated_rmsnorm(o_all, z_ref[:, cols], nw).astype(y_ref.dtype)


def _gdn_scan(hproj, conv_w, ab_rows, a_log, dt_bias, norm_w, batch, seq, d, hb, t_blk):
    heads = d // HEAD_DIM
    hb = min(hb, heads)
    t_blk = min(t_blk, seq)
    w = hb * HEAD_DIM
    nhb = d // w
    nt = seq // t_blk
    gpb = t_blk // GROUP

    def col_spec(part):
        return pl.BlockSpec((t_blk, w), lambda b, h, t: (b * nt + t, part * nhb + h))

    def conv_spec(part):
        return pl.BlockSpec((CONV_TAPS, w), lambda b, h, t: (0, part * nhb + h))

    smem = pl.BlockSpec(memory_space=pltpu.SMEM)
    return pl.pallas_call(
        functools.partial(_gdn_kernel, hb=hb, t_blk=t_blk, heads=heads),
        grid=(batch, nhb, nt),
        in_specs=[smem, smem, col_spec(0), col_spec(1), col_spec(2), col_spec(3),
                  conv_spec(0), conv_spec(1), conv_spec(2),
                  pl.BlockSpec((1, gpb, 2 * heads, GROUP), lambda b, h, t: (b, t, 0, 0)),
                  pl.BlockSpec((1, HEAD_DIM), lambda b, h, t: (0, 0))],
        out_specs=pl.BlockSpec((t_blk, w), lambda b, h, t: (b * nt + t, h)),
        out_shape=jax.ShapeDtypeStruct((batch * seq, d), BF16),
        scratch_shapes=[pltpu.VMEM((hb, HEAD_DIM, HEAD_DIM), F32),
                        pltpu.VMEM((3, hb, SUBLANES, HEAD_DIM), F32)],
        compiler_params=pltpu.CompilerParams(
            dimension_semantics=("parallel", "parallel", "arbitrary"),
            vmem_limit_bytes=VMEM_LIMIT),
        name="gdn_scan",
    )(a_log, dt_bias, hproj, hproj, hproj, hproj, conv_w, conv_w, conv_w, ab_rows, norm_w)


def kernel(x, a_w_in, a_lower_bounds, a_norm_w, a_w_out, b_w_in, b_conv_w, b_a_log, b_dt_bias,
           b_norm_w, b_w_out, ln_g, ln_b):
    batch, seq, d = x.shape
    heads = d // HEAD_DIM
    depth = ln_g.shape[0]
    assert depth == 2 and a_w_in.shape[0] == 1 and b_w_in.shape[0] == 1
    assert a_w_in.shape[2] == 4 * d and b_w_in.shape[2] == 4 * d + 2 * heads
    assert seq % GROUP == 0 and d % HEAD_DIM == 0
    alpha = (2.0 * depth) ** 0.25
    m = batch * seq
    x2 = x.reshape(m, d)

    hproj = _matmul(x2.astype(BF16), a_w_in[0].astype(BF16), tm=1024, tn=512)
    y = _hgrn2_scan(hproj, a_lower_bounds, a_norm_w, batch, seq, d, layer=0, hb=2, t_blk=256)
    x2, x2_bf = _out_proj_ln(y, a_w_out[0].astype(BF16), x2, ln_g[0:1], ln_b[0:1], alpha,
                             tm=256, tk=512)

    w_in = b_w_in[0]
    hproj = _matmul(x2_bf, w_in[:, :4 * d].astype(BF16), tm=1024, tn=512)
    w_gate = jnp.pad(w_in[:, 4 * d:], ((0, 0), (0, HEAD_DIM - 2 * heads)))
    ab = _gate_proj(x2, w_gate, tm=512)[:, :2 * heads]
    ab_rows = ab.reshape(batch, seq // GROUP, GROUP, 2 * heads).transpose(0, 1, 3, 2)
    y = _gdn_scan(hproj, b_conv_w[0], ab_rows, b_a_log[0], b_dt_bias[0], b_norm_w,
                  batch, seq, d, hb=4, t_blk=256)
    x2, _ =_out_proj_ln(y, b_w_out[0].astype(BF16), x2, ln_g[1:2], ln_b[1:2], alpha,
                         tm=256, tk=512)
    return x2.reshape(batch, seq, d)
```

```python
import functools

import jax
import jax.numpy as jnp
from jax import lax
from jax.experimental import pallas as pl
from jax.experimental.pallas import tpu as pltpu

F32 = jnp.float32
BF16 = jnp.bfloat16

HEAD_DIM = 128
CHUNK = 64
SUB = 16
GROUP = 128
CONV_TAPS = 4
SUBLANES = 8
LN_EPS = 1e-5
RMS_EPS = 1e-6
L2_EPS = 1e-6
VMEM_LIMIT = 56 * 1024 * 1024


def _dot(a, b):
    return jnp.dot(a.astype(BF16), b.astype(BF16), preferred_element_type=F32)


def _dot_nt(a, b):
    return lax.dot_general(a.astype(BF16), b.astype(BF16), (((1,), (1,)), ((), ())),
                           preferred_element_type=F32)


def _dot_tn(a, b):
    return lax.dot_general(a.astype(BF16), b.astype(BF16), (((0,), (0,)), ((), ())),
                           preferred_element_type=F32)


def _silu(x):
    return x * jax.nn.sigmoid(x)


def _gated_rmsnorm(o, z, w):
    ms = jnp.mean(o * o, axis=-1, keepdims=True)
    return o * lax.rsqrt(ms + RMS_EPS) * w * _silu(z)


def _matmul_kernel(x_ref, w_ref, o_ref):
    o_ref[...] = jnp.dot(x_ref[...], w_ref[...].astype(BF16), preferred_element_type=F32)


def _matmul(x, w3, n, tm, tn):
    m, k = x.shape
    tm, tn = min(tm, m), min(tn, n)
    return pl.pallas_call(
        _matmul_kernel,
        grid=(m // tm, n // tn),
        in_specs=[pl.BlockSpec((tm, k), lambda i, j: (i, 0)),
                  pl.BlockSpec((None, k, tn), lambda i, j: (0, 0, j))],
        out_specs=pl.BlockSpec((tm, tn), lambda i, j: (i, j)),
        out_shape=jax.ShapeDtypeStruct((m, n), F32),
        compiler_params=pltpu.CompilerParams(
            dimension_semantics=("parallel", "parallel"), vmem_limit_bytes=VMEM_LIMIT),
        name="in_proj",
    )(x, w3)


def _split_bf16(v):
    hi = v.astype(BF16)
    lo = (v - hi.astype(F32)).astype(BF16)
    return hi, lo


def _out_proj_kernel(y_ref, w_ref, x_ref, o_ref, *, alpha):
    o_ref[...] = alpha * x_ref[...] + jnp.dot(y_ref[...], w_ref[...].astype(BF16),
                                              preferred_element_type=F32)


def _out_proj(y, w3, x, alpha, tm, tn):
    m, k = y.shape
    n = w3.shape[2]
    tm, tn = min(tm, m), min(tn, n)
    return pl.pallas_call(
        functools.partial(_out_proj_kernel, alpha=alpha),
        grid=(m // tm, n // tn),
        in_specs=[pl.BlockSpec((tm, k), lambda i, j: (i, 0)),
                  pl.BlockSpec((None, k, tn), lambda i, j: (0, 0, j)),
                  pl.BlockSpec((tm, tn), lambda i, j: (i, j))],
        out_specs=pl.BlockSpec((tm, tn), lambda i, j: (i, j)),
        out_shape=jax.ShapeDtypeStruct((m, n), F32),
        compiler_params=pltpu.CompilerParams(
            dimension_semantics=("parallel", "parallel"), vmem_limit_bytes=VMEM_LIMIT),
        name="out_proj",
    )(y, w3, x)


def _layer_norm(h, g, b):
    mu = jnp.mean(h, axis=-1, keepdims=True)
    c = h - mu
    var = jnp.mean(c * c, axis=-1, keepdims=True)
    return c * lax.rsqrt(var + LN_EPS) * g + b


def _ln_kernel(h_ref, g_ref, b_ref, o_ref):
    o_ref[...] = _layer_norm(h_ref[...], g_ref[...], b_ref[...])


def _ln_gate_kernel(h_ref, g_ref, b_ref, w_ref, o_ref, obf_ref, ab_ref):
    out = _layer_norm(h_ref[...], g_ref[...], b_ref[...])
    o_ref[...] = out
    obf_ref[...] = out.astype(BF16)
    xh, xl = _split_bf16(out)
    wh, wl = _split_bf16(w_ref[...])
    dot = functools.partial(jnp.dot, preferred_element_type=F32)
    ab_ref[...] = dot(xh, wh) + (dot(xl, wh) + dot(xh, wl))


def _ln(h, g, b, tm):
    m, n = h.shape
    tm = min(tm, m)
    row = pl.BlockSpec((tm, n), lambda i: (i, 0))
    vec = pl.BlockSpec((1, n), lambda i: (0, 0))
    return pl.pallas_call(
        _ln_kernel, grid=(m // tm,), in_specs=[row, vec, vec], out_specs=row,
        out_shape=jax.ShapeDtypeStruct((m, n), F32),
        compiler_params=pltpu.CompilerParams(
            dimension_semantics=("parallel",), vmem_limit_bytes=VMEM_LIMIT),
        name="layer_norm",
    )(h, g, b)


def _ln_gate(h, g, b, w_gate, tm):
    m, n = h.shape
    ng = w_gate.shape[1]
    tm = min(tm, m)
    row = pl.BlockSpec((tm, n), lambda i: (i, 0))
    vec = pl.BlockSpec((1, n), lambda i: (0, 0))
    return pl.pallas_call(
        _ln_gate_kernel, grid=(m // tm,),
        in_specs=[row, vec, vec, pl.BlockSpec((n, ng), lambda i: (0, 0))],
        out_specs=[row, row, pl.BlockSpec((tm, ng), lambda i: (i, 0))],
        out_shape=[jax.ShapeDtypeStruct((m, n), F32), jax.ShapeDtypeStruct((m, n), BF16),
                   jax.ShapeDtypeStruct((m, ng), F32)],
        compiler_params=pltpu.CompilerParams(
            dimension_semantics=("parallel",), vmem_limit_bytes=VMEM_LIMIT),
        name="layer_norm_gate",
    )(h, g, b, w_gate)


def _row_of(ref, h, base, s):
    return ref[h, pl.ds(base + s, 1), :]


def _hgrn2_intra(q, k, cum2, k_rows, cum_rows):
    half = SUB // 2
    lane8 = lax.broadcasted_iota(jnp.int32, (half, CHUNK), 1)
    row8 = lax.broadcasted_iota(jnp.int32, (half, CHUNK), 0)
    blocks = []
    for blk in range(CHUNK // SUB):
        r0 = blk * SUB
        halves = [jnp.zeros((half, CHUNK), F32), jnp.zeros((half, CHUNK), F32)]
        if blk > 0:
            start = cum2[r0 - 1:r0]
            q_s = q[r0:r0 + SUB] * jnp.exp2(cum2[r0:r0 + SUB] - start)
            k_s = k * jnp.exp2(jnp.minimum(start - cum2, 0.0))
            off = _dot_nt(q_s, k_s)
            halves = [off[:half], off[half:]]
        for j in range(SUB):
            s = r0 + j
            k_row, cum_row = k_rows(s), cum_rows(s)
            for hf in range(2):
                first = r0 + hf * half
                if s >= first + half:
                    continue
                rows = slice(first, first + half)
                diff = cum2[rows] - cum_row
                if s > first:
                    diff = jnp.minimum(diff, 0.0)
                col = jnp.sum(q[rows] * (k_row * jnp.exp2(diff)), axis=-1, keepdims=True)
                halves[hf] = jnp.where(lane8 == s, col, halves[hf])
        for hf in range(2):
            blocks.append(jnp.where(lane8 <= row8 + (r0 + hf * half), halves[hf], 0.0))
    return jnp.concatenate(blocks, axis=0)


def _hgrn2_kernel(q_ref, f_ref, i_ref, z_ref, lbp_ref, nw_ref, tri_ref, y_ref, st_ref,
                  kbuf_ref, cbuf_ref, *, hb, t_blk, layer):
    @pl.when(pl.program_id(2) == 0)
    def _():
        st_ref[...] = jnp.zeros_like(st_ref)

    tri = tri_ref[...]
    nw = nw_ref[...]
    hrange = range(hb)
    q, k, v, cum2 = [], [], [], []
    for h in hrange:
        cols = slice(h * HEAD_DIM, (h + 1) * HEAD_DIM)
        lbp = lbp_ref[:, cols]
        ex = jnp.exp(lbp - jnp.max(lbp, axis=0, keepdims=True))
        lb = jnp.sum(ex[:layer + 1], axis=0, keepdims=True) / jnp.sum(ex, axis=0, keepdims=True)

        q.append(_silu(q_ref[:, cols]))
        forget = lb + (1.0 - lb) * jax.nn.sigmoid(f_ref[:, cols])
        k.append(1.0 - forget)
        v.append(i_ref[:, cols])

        g_hi, g_lo = _split_bf16(jnp.log2(forget))
        cs = jnp.dot(tri, jnp.concatenate([g_hi, g_lo], axis=1), preferred_element_type=F32)
        cum2.append(cs[:, :HEAD_DIM] + cs[:, HEAD_DIM:])
        kbuf_ref[h] = k[h]
        cbuf_ref[h] = cum2[h]

    st = [st_ref[h] for h in hrange]
    outs = [[] for _ in hrange]
    for c in range(t_blk // CHUNK):
        rows = slice(c * CHUNK, (c + 1) * CHUNK)
        for h in hrange:
            q_c, k_c, v_c, cum_c = q[h][rows], k[h][rows], v[h][rows], cum2[h][rows]
            last = cum_c[CHUNK - 1:CHUNK]
            k_rows = functools.partial(_row_of, kbuf_ref, h, c * CHUNK)
            cum_rows = functools.partial(_row_of, cbuf_ref, h, c * CHUNK)
            a = _hgrn2_intra(q_c, k_c, cum_c, k_rows, cum_rows)
            outs[h].append(_dot(a, v_c) + _dot_nt(q_c * jnp.exp2(cum_c), st[h]))
            st[h] = jnp.exp2(last) * st[h] + _dot_tn(v_c, k_c * jnp.exp2(last - cum_c))
    for h in hrange:
        cols = slice(h * HEAD_DIM, (h + 1) * HEAD_DIM)
        st_ref[h] = st[h]
        o_all = jnp.concatenate(outs[h], axis=0)
        y_ref[:, cols] = _gated_rmsnorm(o_all, z_ref[:, cols], nw).astype(y_ref.dtype)


def _block_diag_tri(t_blk):
    r = lax.broadcasted_iota(jnp.int32, (t_blk, t_blk), 0)
    c = lax.broadcasted_iota(jnp.int32, (t_blk, t_blk), 1)
    return ((c <= r) & (c // CHUNK == r // CHUNK)).astype(BF16)


def _hgrn2_scan(hproj, lb_params, norm_w, batch, seq, d, layer, hb, t_blk):
    hb = min(hb, d // HEAD_DIM)
    t_blk = min(t_blk, seq)
    w = hb * HEAD_DIM
    nhb = d // w
    nt = seq // t_blk

    def col_spec(part):
        return pl.BlockSpec((t_blk, w), lambda b, h, t: (b * nt + t, part * nhb + h))

    return pl.pallas_call(
        functools.partial(_hgrn2_kernel, hb=hb, t_blk=t_blk, layer=layer),
        grid=(batch, nhb, nt),
        in_specs=[col_spec(0), col_spec(1), col_spec(2), col_spec(3),
                  pl.BlockSpec((lb_params.shape[0], w), lambda b, h, t: (0, h)),
                  pl.BlockSpec((1, HEAD_DIM), lambda b, h, t: (0, 0)),
                  pl.BlockSpec((t_blk, t_blk), lambda b, h, t: (0, 0))],
        out_specs=pl.BlockSpec((t_blk, w), lambda b, h, t: (b * nt + t, h)),
        out_shape=jax.ShapeDtypeStruct((batch * seq, d), BF16),
        scratch_shapes=[pltpu.VMEM((hb, HEAD_DIM, HEAD_DIM), F32),
                        pltpu.VMEM((hb, t_blk, HEAD_DIM), F32),
                        pltpu.VMEM((hb, t_blk, HEAD_DIM), F32)],
        compiler_params=pltpu.CompilerParams(
            dimension_semantics=("parallel", "parallel", "arbitrary"),
            vmem_limit_bytes=VMEM_LIMIT),
        name="hgrn2_scan",
    )(hproj, hproj, hproj, hproj, lb_params, norm_w, _block_diag_tri(t_blk))


def _causal_conv_silu(buf_ref, cols, x, w, t_blk):
    buf_ref[pl.ds(SUBLANES, t_blk), cols] = x
    acc = x * w[CONV_TAPS - 1:CONV_TAPS]
    for j in range(1, CONV_TAPS):
        acc = acc + buf_ref[pl.ds(SUBLANES - j, t_blk), cols] * w[CONV_TAPS - 1 - j:CONV_TAPS - j]
    buf_ref[pl.ds(0, SUBLANES), cols] = x[t_blk - SUBLANES:]
    return _silu(acc)


def _l2_normalize(x):
    return x * lax.rsqrt(jnp.sum(x * x, axis=-1, keepdims=True) + L2_EPS)


def _gdn_kernel(alog_ref, dtb_ref, q_ref, k_ref, v_ref, z_ref, cwq_ref, cwk_ref, cwv_ref,
                ab_ref, nw_ref, y_ref, s_ref, stage_ref, *, hb, t_blk, heads):
    @pl.when(pl.program_id(2) == 0)
    def _():
        s_ref[...] = jnp.zeros_like(s_ref)
        stage_ref[:, pl.ds(0, SUBLANES), :] = jnp.zeros((3, SUBLANES, hb * HEAD_DIM), F32)

    nw = nw_ref[...]
    r = lax.broadcasted_iota(jnp.int32, (GROUP, GROUP), 0)
    c = lax.broadcasted_iota(jnp.int32, (GROUP, GROUP), 1)
    same_chunk = (r // CHUNK) == (c // CHUNK)
    eye = r == c
    incl = same_chunk & (c <= r)
    strict = same_chunk & (c < r)
    chunks_per_group = GROUP // CHUNK
    n_groups = t_blk // GROUP
    hrange = range(hb)
    problems = [(h, grp) for h in hrange for grp in range(n_groups)]

    q_all, k_all, v_all = [], [], []
    for h in hrange:
        cols = slice(h * HEAD_DIM, (h + 1) * HEAD_DIM)
        streams = []
        for idx, (x_ref, cw_ref) in enumerate(((q_ref, cwq_ref), (k_ref, cwk_ref), (v_ref, cwv_ref))):
            streams.append(_causal_conv_silu(stage_ref.at[idx], cols, x_ref[:, cols],
                                             cw_ref[:, cols], t_blk))
        q_all.append(_l2_normalize(streams[0]) * (HEAD_DIM ** -0.5))
        k_all.append(_l2_normalize(streams[1]))
        v_all.append(streams[2])

    def rows_of(grp):
        return slice(grp * GROUP, (grp + 1) * GROUP)

    kk = {p: _dot_nt(k_all[p[0]][rows_of(p[1])], k_all[p[0]][rows_of(p[1])]) for p in problems}
    qk = {p: _dot_nt(q_all[p[0]][rows_of(p[1])], k_all[p[0]][rows_of(p[1])]) for p in problems}

    cum_col, e_cum, rhs, inv, power = {}, {}, {}, {}, {}
    for p in problems:
        h, grp = p
        head = pl.program_id(1) * hb + h
        neg_a = -jnp.exp(jnp.full((1, GROUP), alog_ref[head], F32))
        beta_row = jax.nn.sigmoid(ab_ref[0, grp, pl.ds(head, 1), :])
        a_row = ab_ref[0, grp, pl.ds(heads + head, 1), :] + jnp.full((1, GROUP), dtb_ref[head], F32)
        g_row = neg_a * (jnp.maximum(a_row, 0.0) + jnp.log1p(jnp.exp(-jnp.abs(a_row))))

        cum_col[p] = jnp.sum(jnp.where(incl, g_row, 0.0), axis=1, keepdims=True)
        cum_row = jnp.sum(jnp.where(eye, cum_col[p], 0.0), axis=0, keepdims=True)
        beta_col = jnp.sum(jnp.where(eye, beta_row, 0.0), axis=1, keepdims=True)

        decay = jnp.exp(jnp.minimum(cum_col[p] - cum_row, 0.0))
        qk[p] = qk[p] * jnp.where(incl, decay, 0.0)
        power[p] = -(beta_col * kk[p]) * jnp.where(strict, decay, 0.0)
        inv[p] = jnp.where(eye, 1.0, 0.0) + power[p]
        e_cum[p] = jnp.exp(cum_col[p])
        rhs[p] = jnp.concatenate([beta_col * v_all[h][rows_of(grp)],
                                  (beta_col * e_cum[p]) * k_all[h][rows_of(grp)]], axis=1)

    for _ in range(5):
        for p in problems:
            power[p] = _dot(power[p], power[p])
        for p in problems:
            inv[p] = inv[p] + _dot(inv[p], power[p])
    sol = {p: _dot(inv[p], rhs[p]) for p in problems}

    state = [s_ref[h] for h in hrange]
    outs = [[] for _ in hrange]
    for grp in range(n_groups):
        u_parts = [[] for _ in hrange]
        for ci in range(chunks_per_group):
            cr = slice(ci * CHUNK, (ci + 1) * CHUNK)
            last_row = (ci + 1) * CHUNK - 1
            ws = []
            for h in hrange:
                p = (h, grp)
                q_dec = q_all[h][rows_of(grp)][cr] * e_cum[p][cr]
                ws.append(_dot(jnp.concatenate([sol[p][cr, HEAD_DIM:], q_dec], axis=0), state[h]))
            u = [sol[(h, grp)][cr, :HEAD_DIM] - ws[h][:CHUNK] for h in hrange]
            for h in hrange:
                p = (h, grp)
                last = cum_col[p][last_row:last_row + 1]
                k_dec = k_all[h][rows_of(grp)][cr] * jnp.exp(last - cum_col[p][cr])
                state[h] = jnp.exp(last) * state[h] + _dot_tn(k_dec, u[h])
            for h in hrange:
                p = (h, grp)
                u_parts[h].append(u[h])
                rest = [sol[p][(ci + 1) * CHUNK:, :HEAD_DIM]] if ci + 1 < chunks_per_group else []
                u_full = jnp.concatenate(u_parts[h] + rest, axis=0)
                outs[h].append(ws[h][CHUNK:] + _dot(qk[p][cr], u_full))
    for h in hrange:
        cols = slice(h * HEAD_DIM, (h + 1) * HEAD_DIM)
        s_ref[h] = state[h]
        o_all = jnp.concatenate(outs[h], axis=0)
        y_ref[:, cols] = _gated_rmsnorm(o_all, z_ref[:, cols], nw).astype(y_ref.dtype)


def _gdn_scan(hproj, conv_w, ab_rows, a_log, dt_bias, norm_w, batch, seq, d, hb, t_blk):
    heads = d // HEAD_DIM
    hb = min(hb, heads)
    t_blk = min(t_blk, seq)
    w = hb * HEAD_DIM
    nhb = d // w
    nt = seq // t_blk
    gpb = t_blk // GROUP

    def col_spec(part):
        return pl.BlockSpec((t_blk, w), lambda b, h, t: (b * nt + t, part * nhb + h))

    def conv_spec(part):
        return pl.BlockSpec((CONV_TAPS, w), lambda b, h, t: (0, part * nhb + h))

    smem = pl.BlockSpec(memory_space=pltpu.SMEM)
    return pl.pallas_call(
        functools.partial(_gdn_kernel, hb=hb, t_blk=t_blk, heads=heads),
        grid=(batch, nhb, nt),
        in_specs=[smem, smem, col_spec(0), col_spec(1), col_spec(2), col_spec(3),
                  conv_spec(0), conv_spec(1), conv_spec(2),
                  pl.BlockSpec((1, gpb, 2 * heads, GROUP), lambda b, h, t: (b, t, 0, 0)),
                  pl.BlockSpec((1, HEAD_DIM), lambda b, h, t: (0, 0))],
        out_specs=pl.BlockSpec((t_blk, w), lambda b, h, t: (b * nt + t, h)),
        out_shape=jax.ShapeDtypeStruct((batch * seq, d), BF16),
        scratch_shapes=[pltpu.VMEM((hb, HEAD_DIM, HEAD_DIM), F32),
                        pltpu.VMEM((3, SUBLANES + t_blk, w), F32)],
        compiler_params=pltpu.CompilerParams(
            dimension_semantics=("parallel", "parallel", "arbitrary"),
            vmem_limit_bytes=VMEM_LIMIT),
        name="gdn_scan",
    )(a_log, dt_bias, hproj, hproj, hproj, hproj, conv_w, conv_w, conv_w, ab_rows, norm_w)


def kernel(x, a_w_in, a_lower_bounds, a_norm_w, a_w_out, b_w_in, b_conv_w, b_a_log, b_dt_bias,
           b_norm_w, b_w_out, ln_g, ln_b):
    batch, seq, d = x.shape
    heads = d // HEAD_DIM
    depth = ln_g.shape[0]
    assert depth == 2 and a_w_in.shape[0] == 1 and b_w_in.shape[0] == 1
    assert a_w_in.shape[2] == 4 * d and b_w_in.shape[2] == 4 * d + 2 * heads
    assert seq % GROUP == 0 and d % HEAD_DIM == 0
    alpha = (2.0 * depth) ** 0.25
    m = batch * seq
    x2 = x.reshape(m, d)

    hproj = _matmul(x2.astype(BF16), a_w_in, 4 * d, tm=1024, tn=512)
    y = _hgrn2_scan(hproj, a_lower_bounds, a_norm_w, batch, seq, d, layer=0, hb=4, t_blk=256)
    h = _out_proj(y, a_w_out, x2, alpha, tm=1024, tn=512)
    w_gate = jnp.pad(b_w_in[0, :, 4 * d:], ((0, 0), (0, HEAD_DIM - 2 * heads)))
    x2, x2_bf, ab = _ln_gate(h, ln_g[0:1], ln_b[0:1], w_gate, tm=256)

    hproj = _matmul(x2_bf, b_w_in, 4 * d, tm=1024, tn=512)
    ab_rows = ab[:, :2 * heads].reshape(batch, seq // GROUP, GROUP, 2 * heads).transpose(0, 1, 3, 2)
    y = _gdn_scan(hproj, b_conv_w[0], ab_rows, b_a_log[0], b_dt_bias[0], b_norm_w,
                  batch, seq, d, hb=4, t_blk=256)
    h = _out_proj(y, b_w_out, x2, alpha, tm=1024, tn=512)
    return _ln(h, ln_g[1:2], ln_b[1:2], tm=256).reshape(batch, seq, d)
```

```python
import functools

import jax
import jax.numpy as jnp
from jax import lax
from jax.experimental import pallas as pl
from jax.experimental.pallas import tpu as pltpu

F32 = jnp.float32
BF16 = jnp.bfloat16

HEAD_DIM = 128
CHUNK = 64
SUB = 16
GROUP = 128
CONV_TAPS = 4
SUBLANES = 8
LN_EPS = 1e-5
RMS_EPS = 1e-6
L2_EPS = 1e-6
VMEM_LIMIT = 56 * 1024 * 1024


def _dot(a, b):
    return jnp.dot(a.astype(BF16), b.astype(BF16), preferred_element_type=F32)


def _dot_nt(a, b):
    return lax.dot_general(a.astype(BF16), b.astype(BF16), (((1,), (1,)), ((), ())),
                           preferred_element_type=F32)


def _dot_tn(a, b):
    return lax.dot_general(a.astype(BF16), b.astype(BF16), (((0,), (0,)), ((), ())),
                           preferred_element_type=F32)


def _silu(x):
    return x * jax.nn.sigmoid(x)


def _gated_rmsnorm(o, z, w):
    ms = jnp.mean(o * o, axis=-1, keepdims=True)
    return o * lax.rsqrt(ms + RMS_EPS) * w * _silu(z)


def _matmul_kernel(x_ref, w_ref, o_ref, *, w_transposed):
    w = w_ref[...].astype(BF16)
    if w_transposed:
        o_ref[...] = lax.dot_general(x_ref[...], w, (((1,), (1,)), ((), ())),
                                     preferred_element_type=F32)
    else:
        o_ref[...] = jnp.dot(x_ref[...], w, preferred_element_type=F32)


def _matmul(x, w3, n, tm, tn, w_transposed=False):
    m, k = x.shape
    tm, tn = min(tm, m), min(tn, n)
    if w_transposed:
        w_spec = pl.BlockSpec((None, tn, k), lambda i, j: (0, j, 0))
    else:
        w_spec = pl.BlockSpec((None, k, tn), lambda i, j: (0, 0, j))
    return pl.pallas_call(
        functools.partial(_matmul_kernel, w_transposed=w_transposed),
        grid=(m // tm, n // tn),
        in_specs=[pl.BlockSpec((tm, k), lambda i, j: (i, 0)), w_spec],
        out_specs=pl.BlockSpec((tm, tn), lambda i, j: (i, j)),
        out_shape=jax.ShapeDtypeStruct((m, n), F32),
        compiler_params=pltpu.CompilerParams(
            dimension_semantics=("parallel", "parallel"), vmem_limit_bytes=VMEM_LIMIT),
        name="in_proj",
    )(x, w3)


def _split_bf16(v):
    hi = v.astype(BF16)
    lo = (v - hi.astype(F32)).astype(BF16)
    return hi, lo


def _out_proj_kernel(y_ref, w_ref, x_ref, o_ref, *, alpha):
    o_ref[...] = alpha * x_ref[...] + jnp.dot(y_ref[...], w_ref[...].astype(BF16),
                                              preferred_element_type=F32)


def _out_proj(y, w3, x, alpha, tm, tn):
    m, k = y.shape
    n = w3.shape[2]
    tm, tn = min(tm, m), min(tn, n)
    return pl.pallas_call(
        functools.partial(_out_proj_kernel, alpha=alpha),
        grid=(m // tm, n // tn),
        in_specs=[pl.BlockSpec((tm, k), lambda i, j: (i, 0)),
                  pl.BlockSpec((None, k, tn), lambda i, j: (0, 0, j)),
                  pl.BlockSpec((tm, tn), lambda i, j: (i, j))],
        out_specs=pl.BlockSpec((tm, tn), lambda i, j: (i, j)),
        out_shape=jax.ShapeDtypeStruct((m, n), F32),
        compiler_params=pltpu.CompilerParams(
            dimension_semantics=("parallel", "parallel"), vmem_limit_bytes=VMEM_LIMIT),
        name="out_proj",
    )(y, w3, x)


def _layer_norm(h, g, b):
    mu = jnp.mean(h, axis=-1, keepdims=True)
    c = h - mu
    var = jnp.mean(c * c, axis=-1, keepdims=True)
    return c * lax.rsqrt(var + LN_EPS) * g + b


def _ln_kernel(h_ref, g_ref, b_ref, o_ref):
    o_ref[...] = _layer_norm(h_ref[...], g_ref[...], b_ref[...])


def _ln_gate_kernel(h_ref, g_ref, b_ref, w_ref, o_ref, obf_ref, ab_ref):
    out = _layer_norm(h_ref[...], g_ref[...], b_ref[...])
    o_ref[...] = out
    obf_ref[...] = out.astype(BF16)
    xh, xl = _split_bf16(out)
    wh, wl = _split_bf16(w_ref[...])
    dot = functools.partial(jnp.dot, preferred_element_type=F32)
    ab_ref[...] = dot(xh, wh) + (dot(xl, wh) + dot(xh, wl))


def _ln(h, g, b, tm):
    m, n = h.shape
    tm = min(tm, m)
    row = pl.BlockSpec((tm, n), lambda i: (i, 0))
    vec = pl.BlockSpec((1, n), lambda i: (0, 0))
    return pl.pallas_call(
        _ln_kernel, grid=(m // tm,), in_specs=[row, vec, vec], out_specs=row,
        out_shape=jax.ShapeDtypeStruct((m, n), F32),
        compiler_params=pltpu.CompilerParams(
            dimension_semantics=("parallel",), vmem_limit_bytes=VMEM_LIMIT),
        name="layer_norm",
    )(h, g, b)


def _ln_gate(h, g, b, w_gate, tm):
    m, n = h.shape
    ng = w_gate.shape[1]
    tm = min(tm, m)
    row = pl.BlockSpec((tm, n), lambda i: (i, 0))
    vec = pl.BlockSpec((1, n), lambda i: (0, 0))
    return pl.pallas_call(
        _ln_gate_kernel, grid=(m // tm,),
        in_specs=[row, vec, vec, pl.BlockSpec((n, ng), lambda i: (0, 0))],
        out_specs=[row, row, pl.BlockSpec((tm, ng), lambda i: (i, 0))],
        out_shape=[jax.ShapeDtypeStruct((m, n), F32), jax.ShapeDtypeStruct((m, n), BF16),
                   jax.ShapeDtypeStruct((m, ng), F32)],
        compiler_params=pltpu.CompilerParams(
            dimension_semantics=("parallel",), vmem_limit_bytes=VMEM_LIMIT),
        name="layer_norm_gate",
    )(h, g, b, w_gate)


def _row_of(ref, h, base, s):
    return ref[h, pl.ds(base + s, 1), :]


def _hgrn2_intra(q, k, cum2, k_rows, cum_rows):
    half = SUB // 2
    lane8 = lax.broadcasted_iota(jnp.int32, (half, CHUNK), 1)
    row8 = lax.broadcasted_iota(jnp.int32, (half, CHUNK), 0)
    blocks = []
    for blk in range(CHUNK // SUB):
        r0 = blk * SUB
        halves = [jnp.zeros((half, CHUNK), F32), jnp.zeros((half, CHUNK), F32)]
        if blk > 0:
            start = cum2[r0 - 1:r0]
            q_s = q[r0:r0 + SUB] * jnp.exp2(cum2[r0:r0 + SUB] - start)
            k_s = k * jnp.exp2(jnp.minimum(start - cum2, 0.0))
            off = _dot_nt(q_s, k_s)
            halves = [off[:half], off[half:]]
        for j in range(SUB):
            s = r0 + j
            k_row, cum_row = k_rows(s), cum_rows(s)
            for hf in range(2):
                first = r0 + hf * half
                if s >= first + half:
                    continue
                rows = slice(first, first + half)
                diff = cum2[rows] - cum_row
                if s > first:
                    diff = jnp.minimum(diff, 0.0)
                col = jnp.sum(q[rows] * (k_row * jnp.exp2(diff)), axis=-1, keepdims=True)
                halves[hf] = jnp.where(lane8 == s, col, halves[hf])
        for hf in range(2):
            blocks.append(jnp.where(lane8 <= row8 + (r0 + hf * half), halves[hf], 0.0))
    return jnp.concatenate(blocks, axis=0)


def _hgrn2_kernel(q_ref, f_ref, i_ref, z_ref, lbp_ref, nw_ref, tri_ref, y_ref, st_ref,
                  kbuf_ref, cbuf_ref, *, hb, t_blk, layer):
    @pl.when(pl.program_id(2) == 0)
    def _():
        st_ref[...] = jnp.zeros_like(st_ref)

    tri = tri_ref[...]
    nw = nw_ref[...]
    hrange = range(hb)
    q, k, v, cum2 = [], [], [], []
    for h in hrange:
        cols = slice(h * HEAD_DIM, (h + 1) * HEAD_DIM)
        lbp = lbp_ref[:, cols]
        ex = jnp.exp(lbp - jnp.max(lbp, axis=0, keepdims=True))
        lb = jnp.sum(ex[:layer + 1], axis=0, keepdims=True) / jnp.sum(ex, axis=0, keepdims=True)

        q.append(_silu(q_ref[:, cols]))
        forget = lb + (1.0 - lb) * jax.nn.sigmoid(f_ref[:, cols])
        k.append(1.0 - forget)
        v.append(i_ref[:, cols])

        g_hi, g_lo = _split_bf16(jnp.log2(forget))
        cs = jnp.dot(tri, jnp.concatenate([g_hi, g_lo], axis=1), preferred_element_type=F32)
        cum2.append(cs[:, :HEAD_DIM] + cs[:, HEAD_DIM:])
        kbuf_ref[h] = k[h]
        cbuf_ref[h] = cum2[h]

    st = [st_ref[h] for h in hrange]
    outs = [[] for _ in hrange]
    for c in range(t_blk // CHUNK):
        rows = slice(c * CHUNK, (c + 1) * CHUNK)
        for h in hrange:
            q_c, k_c, v_c, cum_c = q[h][rows], k[h][rows], v[h][rows], cum2[h][rows]
            last = cum_c[CHUNK - 1:CHUNK]
            k_rows = functools.partial(_row_of, kbuf_ref, h, c * CHUNK)
            cum_rows = functools.partial(_row_of, cbuf_ref, h, c * CHUNK)
            a = _hgrn2_intra(q_c, k_c, cum_c, k_rows, cum_rows)
            outs[h].append(_dot(a, v_c) + _dot_nt(q_c * jnp.exp2(cum_c), st[h]))
            st[h] = jnp.exp2(last) * st[h] + _dot_tn(v_c, k_c * jnp.exp2(last - cum_c))
    for h in hrange:
        cols = slice(h * HEAD_DIM, (h + 1) * HEAD_DIM)
        st_ref[h] = st[h]
        o_all = jnp.concatenate(outs[h], axis=0)
        y_ref[:, cols] = _gated_rmsnorm(o_all, z_ref[:, cols], nw).astype(y_ref.dtype)


def _block_diag_tri(t_blk):
    r = lax.broadcasted_iota(jnp.int32, (t_blk, t_blk), 0)
    c = lax.broadcasted_iota(jnp.int32, (t_blk, t_blk), 1)
    return ((c <= r) & (c // CHUNK == r // CHUNK)).astype(BF16)


def _hgrn2_scan(hproj, lb_params, norm_w, batch, seq, d, layer, hb, t_blk):
    hb = min(hb, d // HEAD_DIM)
    t_blk = min(t_blk, seq)
    w = hb * HEAD_DIM
    nhb = d // w
    nt = seq // t_blk

    def col_spec(part):
        return pl.BlockSpec((t_blk, w), lambda b, h, t: (b * nt + t, part * nhb + h))

    return pl.pallas_call(
        functools.partial(_hgrn2_kernel, hb=hb, t_blk=t_blk, layer=layer),
        grid=(batch, nhb, nt),
        in_specs=[col_spec(0), col_spec(1), col_spec(2), col_spec(3),
                  pl.BlockSpec((lb_params.shape[0], w), lambda b, h, t: (0, h)),
                  pl.BlockSpec((1, HEAD_DIM), lambda b, h, t: (0, 0)),
                  pl.BlockSpec((t_blk, t_blk), lambda b, h, t: (0, 0))],
        out_specs=pl.BlockSpec((t_blk, w), lambda b, h, t: (b * nt + t, h)),
        out_shape=jax.ShapeDtypeStruct((batch * seq, d), BF16),
        scratch_shapes=[pltpu.VMEM((hb, HEAD_DIM, HEAD_DIM), F32),
                        pltpu.VMEM((hb, t_blk, HEAD_DIM), F32),
                        pltpu.VMEM((hb, t_blk, HEAD_DIM), F32)],
        compiler_params=pltpu.CompilerParams(
            dimension_semantics=("parallel", "parallel", "arbitrary"),
            vmem_limit_bytes=VMEM_LIMIT),
        name="hgrn2_scan",
    )(hproj, hproj, hproj, hproj, lb_params, norm_w, _block_diag_tri(t_blk))


def _causal_conv_silu(buf_ref, cols, x, w, t_blk):
    buf_ref[pl.ds(SUBLANES, t_blk), cols] = x
    acc = x * w[CONV_TAPS - 1:CONV_TAPS]
    for j in range(1, CONV_TAPS):
        acc = acc + buf_ref[pl.ds(SUBLANES - j, t_blk), cols] * w[CONV_TAPS - 1 - j:CONV_TAPS - j]
    buf_ref[pl.ds(0, SUBLANES), cols] = x[t_blk - SUBLANES:]
    return _silu(acc)


def _l2_normalize(x):
    return x * lax.rsqrt(jnp.sum(x * x, axis=-1, keepdims=True) + L2_EPS)


def _gdn_kernel(alog_ref, dtb_ref, q_ref, k_ref, v_ref, z_ref, cwq_ref, cwk_ref, cwv_ref,
                ab_ref, nw_ref, y_ref, s_ref, stage_ref, *, hb, t_blk, heads):
    @pl.when(pl.program_id(2) == 0)
    def _():
        s_ref[...] = jnp.zeros_like(s_ref)
        stage_ref[:, pl.ds(0, SUBLANES), :] = jnp.zeros((3, SUBLANES, hb * HEAD_DIM), F32)

    nw = nw_ref[...]
    r = lax.broadcasted_iota(jnp.int32, (GROUP, GROUP), 0)
    c = lax.broadcasted_iota(jnp.int32, (GROUP, GROUP), 1)
    same_chunk = (r // CHUNK) == (c // CHUNK)
    eye = r == c
    incl = same_chunk & (c <= r)
    strict = same_chunk & (c < r)
    chunks_per_group = GROUP // CHUNK
    n_groups = t_blk // GROUP
    hrange = range(hb)
    problems = [(h, grp) for h in hrange for grp in range(n_groups)]

    q_all, k_all, v_all = [], [], []
    for h in hrange:
        cols = slice(h * HEAD_DIM, (h + 1) * HEAD_DIM)
        streams = []
        for idx, (x_ref, cw_ref) in enumerate(((q_ref, cwq_ref), (k_ref, cwk_ref), (v_ref, cwv_ref))):
            streams.append(_causal_conv_silu(stage_ref.at[idx], cols, x_ref[:, cols],
                                             cw_ref[:, cols], t_blk))
        q_all.append(_l2_normalize(streams[0]) * (HEAD_DIM ** -0.5))
        k_all.append(_l2_normalize(streams[1]))
        v_all.append(streams[2])

    def rows_of(grp):
        return slice(grp * GROUP, (grp + 1) * GROUP)

    kk = {p: _dot_nt(k_all[p[0]][rows_of(p[1])], k_all[p[0]][rows_of(p[1])]) for p in problems}
    qk = {p: _dot_nt(q_all[p[0]][rows_of(p[1])], k_all[p[0]][rows_of(p[1])]) for p in problems}

    cum_col, e_cum, rhs, inv, power = {}, {}, {}, {}, {}
    for p in problems:
        h, grp = p
        head = pl.program_id(1) * hb + h
        neg_a = -jnp.exp(jnp.full((1, GROUP), alog_ref[head], F32))
        beta_row = jax.nn.sigmoid(ab_ref[0, grp, pl.ds(head, 1), :])
        a_row = ab_ref[0, grp, pl.ds(heads + head, 1), :] + jnp.full((1, GROUP), dtb_ref[head], F32)
        g_row = neg_a * (jnp.maximum(a_row, 0.0) + jnp.log1p(jnp.exp(-jnp.abs(a_row))))

        cum_col[p] = jnp.sum(jnp.where(incl, g_row, 0.0), axis=1, keepdims=True)
        cum_row = jnp.sum(jnp.where(eye, cum_col[p], 0.0), axis=0, keepdims=True)
        beta_col = jnp.sum(jnp.where(eye, beta_row, 0.0), axis=1, keepdims=True)

        decay = jnp.exp(jnp.minimum(cum_col[p] - cum_row, 0.0))
        qk[p] = qk[p] * jnp.where(incl, decay, 0.0)
        power[p] = -(beta_col * kk[p]) * jnp.where(strict, decay, 0.0)
        inv[p] = jnp.where(eye, 1.0, 0.0) + power[p]
        e_cum[p] = jnp.exp(cum_col[p])
        rhs[p] = jnp.concatenate([beta_col * v_all[h][rows_of(grp)],
                                  (beta_col * e_cum[p]) * k_all[h][rows_of(grp)]], axis=1)

    for _ in range(5):
        for p in problems:
            power[p] = _dot(power[p], power[p])
        for p in problems:
            inv[p] = inv[p] + _dot(inv[p], power[p])
    sol = {p: _dot(inv[p], rhs[p]) for p in problems}

    state = [s_ref[h] for h in hrange]
    outs = [[] for _ in hrange]
    for grp in range(n_groups):
        u_parts = [[] for _ in hrange]
        for ci in range(chunks_per_group):
            cr = slice(ci * CHUNK, (ci + 1) * CHUNK)
            last_row = (ci + 1) * CHUNK - 1
            ws = []
            for h in hrange:
                p = (h, grp)
                q_dec = q_all[h][rows_of(grp)][cr] * e_cum[p][cr]
                ws.append(_dot(jnp.concatenate([sol[p][cr, HEAD_DIM:], q_dec], axis=0), state[h]))
            u = [sol[(h, grp)][cr, :HEAD_DIM] - ws[h][:CHUNK] for h in hrange]
            for h in hrange:
                p = (h, grp)
                last = cum_col[p][last_row:last_row + 1]
                k_dec = k_all[h][rows_of(grp)][cr] * jnp.exp(last - cum_col[p][cr])
                state[h] = jnp.exp(last) * state[h] + _dot_tn(k_dec, u[h])
            for h in hrange:
                p = (h, grp)
                u_parts[h].append(u[h])
                rest = [sol[p][(ci + 1) * CHUNK:, :HEAD_DIM]] if ci + 1 < chunks_per_group else []
                u_full = jnp.concatenate(u_parts[h] + rest, axis=0)
                outs[h].append(ws[h][CHUNK:] + _dot(qk[p][cr], u_full))
    for h in hrange:
        cols = slice(h * HEAD_DIM, (h + 1) * HEAD_DIM)
        s_ref[h] = state[h]
        o_all = jnp.concatenate(outs[h], axis=0)
        y_ref[:, cols] = _gated_rmsnorm(o_all, z_ref[:, cols], nw).astype(y_ref.dtype)


def _gdn_scan(hproj, conv_w, ab_rows, a_log, dt_bias, norm_w, batch, seq, d, hb, t_blk):
    heads = d // HEAD_DIM
    hb = min(hb, heads)
    t_blk = min(t_blk, seq)
    w = hb * HEAD_DIM
    nhb = d // w
    nt = seq // t_blk
    gpb = t_blk // GROUP

    def col_spec(part):
        return pl.BlockSpec((t_blk, w), lambda b, h, t: (b * nt + t, part * nhb + h))

    def conv_spec(part):
        return pl.BlockSpec((CONV_TAPS, w), lambda b, h, t: (0, part * nhb + h))

    smem = pl.BlockSpec(memory_space=pltpu.SMEM)
    return pl.pallas_call(
        functools.partial(_gdn_kernel, hb=hb, t_blk=t_blk, heads=heads),
        grid=(batch, nhb, nt),
        in_specs=[smem, smem, col_spec(0), col_spec(1), col_spec(2), col_spec(3),
                  conv_spec(0), conv_spec(1), conv_spec(2),
                  pl.BlockSpec((1, gpb, 2 * heads, GROUP), lambda b, h, t: (b, t, 0, 0)),
                  pl.BlockSpec((1, HEAD_DIM), lambda b, h, t: (0, 0))],
        out_specs=pl.BlockSpec((t_blk, w), lambda b, h, t: (b * nt + t, h)),
        out_shape=jax.ShapeDtypeStruct((batch * seq, d), BF16),
        scratch_shapes=[pltpu.VMEM((hb, HEAD_DIM, HEAD_DIM), F32),
                        pltpu.VMEM((3, SUBLANES + t_blk, w), F32)],
        compiler_params=pltpu.CompilerParams(
            dimension_semantics=("parallel", "parallel", "arbitrary"),
            vmem_limit_bytes=VMEM_LIMIT),
        name="gdn_scan",
    )(a_log, dt_bias, hproj, hproj, hproj, hproj, conv_w, conv_w, conv_w, ab_rows, norm_w)


def kernel(x, a_w_in, a_lower_bounds, a_norm_w, a_w_out, b_w_in, b_conv_w, b_a_log, b_dt_bias,
           b_norm_w, b_w_out, ln_g, ln_b):
    batch, seq, d = x.shape
    heads = d // HEAD_DIM
    depth = ln_g.shape[0]
    assert depth == 2 and a_w_in.shape[0] == 1 and b_w_in.shape[0] == 1
    assert a_w_in.shape[2] == 4 * d and b_w_in.shape[2] == 4 * d + 2 * heads
    assert seq % GROUP == 0 and d % HEAD_DIM == 0
    alpha = (2.0 * depth) ** 0.25
    m = batch * seq
    x2 = x.reshape(m, d)

    hproj = _matmul(x2.astype(BF16), a_w_in, 4 * d, tm=1024, tn=512)
    y = _hgrn2_scan(hproj, a_lower_bounds, a_norm_w, batch, seq, d, layer=0, hb=8, t_blk=256)
    h = _out_proj(y, a_w_out, x2, alpha, tm=1024, tn=512)
    b_w_in_t = jnp.swapaxes(b_w_in, 1, 2)
    w_gate = jnp.pad(b_w_in_t[0, 4 * d:, :].T, ((0, 0), (0, HEAD_DIM - 2 * heads)))
    x2, x2_bf, ab = _ln_gate(h, ln_g[0:1], ln_b[0:1], w_gate, tm=256)

    hproj = _matmul(x2_bf, b_w_in_t, 4 * d, tm=1024, tn=512, w_transposed=True)
    ab_rows = ab[:, :2 * heads].reshape(batch, seq // GROUP, GROUP, 2 * heads).transpose(0, 1, 3, 2)
    y = _gdn_scan(hproj, b_conv_w[0], ab_rows, b_a_log[0], b_dt_bias[0], b_norm_w,
                  batch, seq, d, hb=8, t_blk=256)
    h = _out_proj(y, b_w_out, x2, alpha, tm=1024, tn=512)
    return _ln(h, ln_g[1:2], ln_b[1:2], tm=256).reshape(batch, seq, d)
```

```python
import functools

import jax
import jax.numpy as jnp
from jax import lax
from jax.experimental import pallas as pl
from jax.experimental.pallas import tpu as pltpu

F32 = jnp.float32
BF16 = jnp.bfloat16

HEAD_DIM = 128
CHUNK = 64
SUB = 8
GROUP = 128
CONV_TAPS = 4
SUBLANES = 8
LN_EPS = 1e-5
RMS_EPS = 1e-6
L2_EPS = 1e-6
VMEM_LIMIT = 56 * 1024 * 1024


def _dot(a, b):
    return jnp.dot(a.astype(BF16), b.astype(BF16), preferred_element_type=F32)


def _dot_nt(a, b):
    return lax.dot_general(a.astype(BF16), b.astype(BF16), (((1,), (1,)), ((), ())),
                           preferred_element_type=F32)


def _dot_tn(a, b):
    return lax.dot_general(a.astype(BF16), b.astype(BF16), (((0,), (0,)), ((), ())),
                           preferred_element_type=F32)


def _silu(x):
    return x * jax.nn.sigmoid(x)


def _gated_rmsnorm(o, z, w):
    ms = jnp.mean(o * o, axis=-1, keepdims=True)
    return o * lax.rsqrt(ms + RMS_EPS) * w * _silu(z)


def _matmul_kernel(x_ref, w_ref, o_ref, *, w_transposed):
    w = w_ref[...].astype(BF16)
    if w_transposed:
        o_ref[...] = lax.dot_general(x_ref[...], w, (((1,), (1,)), ((), ())),
                                     preferred_element_type=F32)
    else:
        o_ref[...] = jnp.dot(x_ref[...], w, preferred_element_type=F32)


def _matmul(x, w3, n, tm, tn, w_transposed=False):
    m, k = x.shape
    tm, tn = min(tm, m), min(tn, n)
    if w_transposed:
        w_spec = pl.BlockSpec((None, tn, k), lambda i, j: (0, j, 0))
    else:
        w_spec = pl.BlockSpec((None, k, tn), lambda i, j: (0, 0, j))
    return pl.pallas_call(
        functools.partial(_matmul_kernel, w_transposed=w_transposed),
        grid=(m // tm, n // tn),
        in_specs=[pl.BlockSpec((tm, k), lambda i, j: (i, 0)), w_spec],
        out_specs=pl.BlockSpec((tm, tn), lambda i, j: (i, j)),
        out_shape=jax.ShapeDtypeStruct((m, n), F32),
        compiler_params=pltpu.CompilerParams(
            dimension_semantics=("parallel", "parallel"), vmem_limit_bytes=VMEM_LIMIT),
        name="in_proj",
    )(x, w3)


def _split_bf16(v):
    hi = v.astype(BF16)
    lo = (v - hi.astype(F32)).astype(BF16)
    return hi, lo


def _out_proj_kernel(y_ref, w_ref, x_ref, o_ref, *, alpha):
    o_ref[...] = alpha * x_ref[...] + jnp.dot(y_ref[...], w_ref[...].astype(BF16),
                                              preferred_element_type=F32)


def _out_proj(y, w3, x, alpha, tm, tn):
    m, k = y.shape
    n = w3.shape[2]
    tm, tn = min(tm, m), min(tn, n)
    return pl.pallas_call(
        functools.partial(_out_proj_kernel, alpha=alpha),
        grid=(m // tm, n // tn),
        in_specs=[pl.BlockSpec((tm, k), lambda i, j: (i, 0)),
                  pl.BlockSpec((None, k, tn), lambda i, j: (0, 0, j)),
                  pl.BlockSpec((tm, tn), lambda i, j: (i, j))],
        out_specs=pl.BlockSpec((tm, tn), lambda i, j: (i, j)),
        out_shape=jax.ShapeDtypeStruct((m, n), F32),
        compiler_params=pltpu.CompilerParams(
            dimension_semantics=("parallel", "parallel"), vmem_limit_bytes=VMEM_LIMIT),
        name="out_proj",
    )(y, w3, x)


def _layer_norm(h, g, b):
    mu = jnp.mean(h, axis=-1, keepdims=True)
    c = h - mu
    var = jnp.mean(c * c, axis=-1, keepdims=True)
    return c * lax.rsqrt(var + LN_EPS) * g + b


def _ln_kernel(h_ref, g_ref, b_ref, o_ref):
    o_ref[...] = _layer_norm(h_ref[...], g_ref[...], b_ref[...])


def _ln_gate_kernel(h_ref, g_ref, b_ref, w_ref, o_ref, obf_ref, ab_ref):
    out = _layer_norm(h_ref[...], g_ref[...], b_ref[...])
    o_ref[...] = out
    obf_ref[...] = out.astype(BF16)
    xh, xl = _split_bf16(out)
    wh, wl = _split_bf16(w_ref[...])
    dot = functools.partial(jnp.dot, preferred_element_type=F32)
    ab_ref[...] = dot(xh, wh) + (dot(xl, wh) + dot(xh, wl))


def _ln(h, g, b, tm):
    m, n = h.shape
    tm = min(tm, m)
    row = pl.BlockSpec((tm, n), lambda i: (i, 0))
    vec = pl.BlockSpec((1, n), lambda i: (0, 0))
    return pl.pallas_call(
        _ln_kernel, grid=(m // tm,), in_specs=[row, vec, vec], out_specs=row,
        out_shape=jax.ShapeDtypeStruct((m, n), F32),
        compiler_params=pltpu.CompilerParams(
            dimension_semantics=("parallel",), vmem_limit_bytes=VMEM_LIMIT),
        name="layer_norm",
    )(h, g, b)


def _ln_gate(h, g, b, w_gate, tm):
    m, n = h.shape
    ng = w_gate.shape[1]
    tm = min(tm, m)
    row = pl.BlockSpec((tm, n), lambda i: (i, 0))
    vec = pl.BlockSpec((1, n), lambda i: (0, 0))
    return pl.pallas_call(
        _ln_gate_kernel, grid=(m // tm,),
        in_specs=[row, vec, vec, pl.BlockSpec((n, ng), lambda i: (0, 0))],
        out_specs=[row, row, pl.BlockSpec((tm, ng), lambda i: (i, 0))],
        out_shape=[jax.ShapeDtypeStruct((m, n), F32), jax.ShapeDtypeStruct((m, n), BF16),
                   jax.ShapeDtypeStruct((m, ng), F32)],
        compiler_params=pltpu.CompilerParams(
            dimension_semantics=("parallel",), vmem_limit_bytes=VMEM_LIMIT),
        name="layer_norm_gate",
    )(h, g, b, w_gate)


def _row_of(ref, h, base, s):
    return ref[h, pl.ds(base + s, 1), :]


def _hgrn2_offdiag(q, k, cum2):
    offs = []
    for blk in range(1, CHUNK // SUB):
        r0 = blk * SUB
        start = cum2[r0 - 1:r0]
        q_s = q[r0:r0 + SUB] * jnp.exp2(cum2[r0:r0 + SUB] - start)
        k_s = jnp.concatenate([k[:r0] * jnp.exp2(start - cum2[:r0]),
                               jnp.zeros((CHUNK - r0, HEAD_DIM), F32)], axis=0)
        offs.append(_dot_nt(q_s, k_s))
    return offs


def _hgrn2_intra(q, cum2, offs, k_rows, cum_rows):
    lane = lax.broadcasted_iota(jnp.int32, (SUB, CHUNK), 1)
    row = lax.broadcasted_iota(jnp.int32, (SUB, CHUNK), 0)
    blocks = []
    for blk in range(CHUNK // SUB):
        r0 = blk * SUB
        q_b, cum_b = q[r0:r0 + SUB], cum2[r0:r0 + SUB]
        a = jnp.zeros((SUB, CHUNK), F32)
        for j in range(SUB):
            s = r0 + j
            diff = cum_b - cum_rows(s)
            if j > 0:
                diff = jnp.minimum(diff, 0.0)
            col = jnp.sum(q_b * (k_rows(s) * jnp.exp2(diff)), axis=-1, keepdims=True)
            a = jnp.where(lane == s, col, a)
        a = jnp.where(lane <= row + r0, a, 0.0)
        if blk > 0:
            a = jnp.where(lane < r0, offs[blk - 1], a)
        blocks.append(a)
    return jnp.concatenate(blocks, axis=0)


def _hgrn2_kernel(q_ref, f_ref, i_ref, z_ref, lbp_ref, nw_ref, tri_ref, y_ref, st_ref,
                  kbuf_ref, cbuf_ref, *, hb, t_blk, layer):
    @pl.when(pl.program_id(2) == 0)
    def _():
        st_ref[...] = jnp.zeros_like(st_ref)

    tri = tri_ref[...]
    nw = nw_ref[...]
    hrange = range(hb)
    q, k, v, cum2 = [], [], [], []
    for h in hrange:
        cols = slice(h * HEAD_DIM, (h + 1) * HEAD_DIM)
        lbp = lbp_ref[:, cols]
        ex = jnp.exp(lbp - jnp.max(lbp, axis=0, keepdims=True))
        lb = jnp.sum(ex[:layer + 1], axis=0, keepdims=True) / jnp.sum(ex, axis=0, keepdims=True)

        q.append(_silu(q_ref[:, cols]))
        forget = lb + (1.0 - lb) * jax.nn.sigmoid(f_ref[:, cols])
        k.append(1.0 - forget)
        v.append(i_ref[:, cols])

        g_hi, g_lo = _split_bf16(jnp.log2(forget))
        cs = jnp.dot(tri, jnp.concatenate([g_hi, g_lo], axis=1), preferred_element_type=F32)
        cum2.append(cs[:, :HEAD_DIM] + cs[:, HEAD_DIM:])
        kbuf_ref[h] = k[h]
        cbuf_ref[h] = cum2[h]

    st = [st_ref[h] for h in hrange]
    outs = [[] for _ in hrange]
    n_chunks = t_blk // CHUNK
    chunk_rows = [slice(c * CHUNK, (c + 1) * CHUNK) for c in range(n_chunks)]
    offs = {(c, h): _hgrn2_offdiag(q[h][chunk_rows[c]], k[h][chunk_rows[c]], cum2[h][chunk_rows[c]])
            for c in range(n_chunks) for h in hrange}
    for c in range(n_chunks):
        rows = chunk_rows[c]
        for h in hrange:
            q_c, k_c, v_c, cum_c = q[h][rows], k[h][rows], v[h][rows], cum2[h][rows]
            last = cum_c[CHUNK - 1:CHUNK]
            k_rows = functools.partial(_row_of, kbuf_ref, h, c * CHUNK)
            cum_rows = functools.partial(_row_of, cbuf_ref, h, c * CHUNK)
            a = _hgrn2_intra(q_c, cum_c, offs[c, h], k_rows, cum_rows)
            outs[h].append(_dot(a, v_c) + _dot_nt(q_c * jnp.exp2(cum_c), st[h]))
            st[h] = jnp.exp2(last) * st[h] + _dot_tn(v_c, k_c * jnp.exp2(last - cum_c))
    for h in hrange:
        cols = slice(h * HEAD_DIM, (h + 1) * HEAD_DIM)
        st_ref[h] = st[h]
        o_all = jnp.concatenate(outs[h], axis=0)
        y_ref[:, cols] = _gated_rmsnorm(o_all, z_ref[:, cols], nw).astype(y_ref.dtype)


def _block_diag_tri(t_blk):
    r = lax.broadcasted_iota(jnp.int32, (t_blk, t_blk), 0)
    c = lax.broadcasted_iota(jnp.int32, (t_blk, t_blk), 1)
    return ((c <= r) & (c // CHUNK == r // CHUNK)).astype(BF16)


def _hgrn2_scan(hproj, lb_params, norm_w, batch, seq, d, layer, hb, t_blk):
    hb = min(hb, d // HEAD_DIM)
    t_blk = min(t_blk, seq)
    w = hb * HEAD_DIM
    nhb = d // w
    nt = seq // t_blk

    def col_spec(part):
        return pl.BlockSpec((t_blk, w), lambda b, h, t: (b * nt + t, part * nhb + h))

    return pl.pallas_call(
        functools.partial(_hgrn2_kernel, hb=hb, t_blk=t_blk, layer=layer),
        grid=(batch, nhb, nt),
        in_specs=[col_spec(0), col_spec(1), col_spec(2), col_spec(3),
                  pl.BlockSpec((lb_params.shape[0], w), lambda b, h, t: (0, h)),
                  pl.BlockSpec((1, HEAD_DIM), lambda b, h, t: (0, 0)),
                  pl.BlockSpec((t_blk, t_blk), lambda b, h, t: (0, 0))],
        out_specs=pl.BlockSpec((t_blk, w), lambda b, h, t: (b * nt + t, h)),
        out_shape=jax.ShapeDtypeStruct((batch * seq, d), BF16),
        scratch_shapes=[pltpu.VMEM((hb, HEAD_DIM, HEAD_DIM), F32),
                        pltpu.VMEM((hb, t_blk, HEAD_DIM), F32),
                        pltpu.VMEM((hb, t_blk, HEAD_DIM), F32)],
        compiler_params=pltpu.CompilerParams(
            dimension_semantics=("parallel", "parallel", "arbitrary"),
            vmem_limit_bytes=VMEM_LIMIT),
        name="hgrn2_scan",
    )(hproj, hproj, hproj, hproj, lb_params, norm_w, _block_diag_tri(t_blk))


def _causal_conv_silu(buf_ref, cols, x, w, t_blk):
    buf_ref[pl.ds(SUBLANES, t_blk), cols] = x
    acc = x * w[CONV_TAPS - 1:CONV_TAPS]
    for j in range(1, CONV_TAPS):
        acc = acc + buf_ref[pl.ds(SUBLANES - j, t_blk), cols] * w[CONV_TAPS - 1 - j:CONV_TAPS - j]
    buf_ref[pl.ds(0, SUBLANES), cols] = x[t_blk - SUBLANES:]
    return _silu(acc)


def _l2_normalize(x):
    return x * lax.rsqrt(jnp.sum(x * x, axis=-1, keepdims=True) + L2_EPS)


def _gdn_kernel(alog_ref, dtb_ref, q_ref, k_ref, v_ref, z_ref, cwq_ref, cwk_ref, cwv_ref,
                ab_ref, nw_ref, y_ref, s_ref, stage_ref, *, hb, t_blk, heads):
    @pl.when(pl.program_id(2) == 0)
    def _():
        s_ref[...] = jnp.zeros_like(s_ref)
        stage_ref[:, pl.ds(0, SUBLANES), :] = jnp.zeros((3, SUBLANES, hb * HEAD_DIM), F32)

    nw = nw_ref[...]
    r = lax.broadcasted_iota(jnp.int32, (GROUP, GROUP), 0)
    c = lax.broadcasted_iota(jnp.int32, (GROUP, GROUP), 1)
    same_chunk = (r // CHUNK) == (c // CHUNK)
    eye = r == c
    incl = same_chunk & (c <= r)
    strict = same_chunk & (c < r)
    chunks_per_group = GROUP // CHUNK
    n_groups = t_blk // GROUP
    hrange = range(hb)
    problems = [(h, grp) for h in hrange for grp in range(n_groups)]

    q_all, k_all, v_all = [], [], []
    for h in hrange:
        cols = slice(h * HEAD_DIM, (h + 1) * HEAD_DIM)
        streams = []
        for idx, (x_ref, cw_ref) in enumerate(((q_ref, cwq_ref), (k_ref, cwk_ref), (v_ref, cwv_ref))):
            streams.append(_causal_conv_silu(stage_ref.at[idx], cols, x_ref[:, cols],
                                             cw_ref[:, cols], t_blk))
        q_all.append(_l2_normalize(streams[0]) * (HEAD_DIM ** -0.5))
        k_all.append(_l2_normalize(streams[1]))
        v_all.append(streams[2])

    def rows_of(grp):
        return slice(grp * GROUP, (grp + 1) * GROUP)

    kk = {p: _dot_nt(k_all[p[0]][rows_of(p[1])], k_all[p[0]][rows_of(p[1])]) for p in problems}
    qk = {p: _dot_nt(q_all[p[0]][rows_of(p[1])], k_all[p[0]][rows_of(p[1])]) for p in problems}

    cum_col, e_cum, rhs, inv, power = {}, {}, {}, {}, {}
    for p in problems:
        h, grp = p
        head = pl.program_id(1) * hb + h
        neg_a = -jnp.exp(jnp.full((1, GROUP), alog_ref[head], F32))
        beta_row = jax.nn.sigmoid(ab_ref[0, grp, pl.ds(head, 1), :])
        a_row = ab_ref[0, grp, pl.ds(heads + head, 1), :] + jnp.full((1, GROUP), dtb_ref[head], F32)
        g_row = neg_a * (jnp.maximum(a_row, 0.0) + jnp.log1p(jnp.exp(-jnp.abs(a_row))))

        cum_col[p] = jnp.sum(jnp.where(incl, g_row, 0.0), axis=1, keepdims=True)
        cum_row = jnp.sum(jnp.where(eye, cum_col[p], 0.0), axis=0, keepdims=True)
        beta_col = jnp.sum(jnp.where(eye, beta_row, 0.0), axis=1, keepdims=True)

        decay = jnp.exp(jnp.minimum(cum_col[p] - cum_row, 0.0))
        qk[p] = qk[p] * jnp.where(incl, decay, 0.0)
        power[p] = -(beta_col * kk[p]) * jnp.where(strict, decay, 0.0)
        inv[p] = jnp.where(eye, 1.0, 0.0) + power[p]
        e_cum[p] = jnp.exp(cum_col[p])
        rhs[p] = jnp.concatenate([beta_col * v_all[h][rows_of(grp)],
                                  (beta_col * e_cum[p]) * k_all[h][rows_of(grp)]], axis=1)

    for _ in range(5):
        for p in problems:
            power[p] = _dot(power[p], power[p])
        for p in problems:
            inv[p] = inv[p] + _dot(inv[p], power[p])
    sol = {p: _dot(inv[p], rhs[p]) for p in problems}

    state = [s_ref[h] for h in hrange]
    outs = [[] for _ in hrange]
    for grp in range(n_groups):
        u_parts = [[] for _ in hrange]
        for ci in range(chunks_per_group):
            cr = slice(ci * CHUNK, (ci + 1) * CHUNK)
            last_row = (ci + 1) * CHUNK - 1
            ws = []
            for h in hrange:
                p = (h, grp)
                q_dec = q_all[h][rows_of(grp)][cr] * e_cum[p][cr]
                ws.append(_dot(jnp.concatenate([sol[p][cr, HEAD_DIM:], q_dec], axis=0), state[h]))
            u = [sol[(h, grp)][cr, :HEAD_DIM] - ws[h][:CHUNK] for h in hrange]
            for h in hrange:
                p = (h, grp)
                last = cum_col[p][last_row:last_row + 1]
                k_dec = k_all[h][rows_of(grp)][cr] * jnp.exp(last - cum_col[p][cr])
                state[h] = jnp.exp(last) * state[h] + _dot_tn(k_dec, u[h])
            for h in hrange:
                p = (h, grp)
                u_parts[h].append(u[h])
                rest = [sol[p][(ci + 1) * CHUNK:, :HEAD_DIM]] if ci + 1 < chunks_per_group else []
                u_full = jnp.concatenate(u_parts[h] + rest, axis=0)
                outs[h].append(ws[h][CHUNK:] + _dot(qk[p][cr], u_full))
    for h in hrange:
        cols = slice(h * HEAD_DIM, (h + 1) * HEAD_DIM)
        s_ref[h] = state[h]
        o_all = jnp.concatenate(outs[h], axis=0)
        y_ref[:, cols] = _gated_rmsnorm(o_all, z_ref[:, cols], nw).astype(y_ref.dtype)


def _gdn_scan(hproj, conv_w, ab_rows, a_log, dt_bias, norm_w, batch, seq, d, hb, t_blk):
    heads = d // HEAD_DIM
    hb = min(hb, heads)
    t_blk = min(t_blk, seq)
    w = hb * HEAD_DIM
    nhb = d // w
    nt = seq // t_blk
    gpb = t_blk // GROUP

    def col_spec(part):
        return pl.BlockSpec((t_blk, w), lambda b, h, t: (b * nt + t, part * nhb + h))

    def conv_spec(part):
        return pl.BlockSpec((CONV_TAPS, w), lambda b, h, t: (0, part * nhb + h))

    smem = pl.BlockSpec(memory_space=pltpu.SMEM)
    return pl.pallas_call(
        functools.partial(_gdn_kernel, hb=hb, t_blk=t_blk, heads=heads),
        grid=(batch, nhb, nt),
        in_specs=[smem, smem, col_spec(0), col_spec(1), col_spec(2), col_spec(3),
                  conv_spec(0), conv_spec(1), conv_spec(2),
                  pl.BlockSpec((1, gpb, 2 * heads, GROUP), lambda b, h, t: (b, t, 0, 0)),
                  pl.BlockSpec((1, HEAD_DIM), lambda b, h, t: (0, 0))],
        out_specs=pl.BlockSpec((t_blk, w), lambda b, h, t: (b * nt + t, h)),
        out_shape=jax.ShapeDtypeStruct((batch * seq, d), BF16),
        scratch_shapes=[pltpu.VMEM((hb, HEAD_DIM, HEAD_DIM), F32),
                        pltpu.VMEM((3, SUBLANES + t_blk, w), F32)],
        compiler_params=pltpu.CompilerParams(
            dimension_semantics=("parallel", "parallel", "arbitrary"),
            vmem_limit_bytes=VMEM_LIMIT),
        name="gdn_scan",
    )(a_log, dt_bias, hproj, hproj, hproj, hproj, conv_w, conv_w, conv_w, ab_rows, norm_w)


def kernel(x, a_w_in, a_lower_bounds, a_norm_w, a_w_out, b_w_in, b_conv_w, b_a_log, b_dt_bias,
           b_norm_w, b_w_out, ln_g, ln_b):
    batch, seq, d = x.shape
    heads = d // HEAD_DIM
    depth = ln_g.shape[0]
    assert depth == 2 and a_w_in.shape[0] == 1 and b_w_in.shape[0] == 1
    assert a_w_in.shape[2] == 4 * d and b_w_in.shape[2] == 4 * d + 2 * heads
    assert seq % GROUP == 0 and d % HEAD_DIM == 0
    alpha = (2.0 * depth) ** 0.25
    m = batch * seq
    x2 = x.reshape(m, d)

    hproj = _matmul(x2.astype(BF16), a_w_in, 4 * d, tm=1024, tn=512)
    y = _hgrn2_scan(hproj, a_lower_bounds, a_norm_w, batch, seq, d, layer=0, hb=8, t_blk=256)
    h = _out_proj(y, a_w_out, x2, alpha, tm=1024, tn=512)
    b_w_in_t = jnp.swapaxes(b_w_in, 1, 2)
    w_gate = jnp.pad(b_w_in_t[0, 4 * d:, :].T, ((0, 0), (0, HEAD_DIM - 2 * heads)))
    x2, x2_bf, ab = _ln_gate(h, ln_g[0:1], ln_b[0:1], w_gate, tm=256)

    hproj = _matmul(x2_bf, b_w_in_t, 4 * d, tm=1024, tn=512, w_transposed=True)
    ab_rows = ab[:, :2 * heads].reshape(batch, seq // GROUP, GROUP, 2 * heads).transpose(0, 1, 3, 2)
    y = _gdn_scan(hproj, b_conv_w[0], ab_rows, b_a_log[0], b_dt_bias[0], b_norm_w,
                  batch, seq, d, hb=8, t_blk=256)
    h = _out_proj(y, b_w_out, x2, alpha, tm=1024, tn=512)
    return _ln(h, ln_g[1:2], ln_b[1:2], tm=256).reshape(batch, seq, d)
```

```python
import functools

import jax
import jax.numpy as jnp
from jax import lax
from jax.experimental import pallas as pl
from jax.experimental.pallas import tpu as pltpu

F32 = jnp.float32
BF16 = jnp.bfloat16

HEAD_DIM = 128
CHUNK = 64
SUB = 8
GROUP = 128
CONV_TAPS = 4
SUBLANES = 8
LN_EPS = 1e-5
RMS_EPS = 1e-6
L2_EPS = 1e-6
VMEM_LIMIT = 56 * 1024 * 1024


def _dot(a, b):
    return jnp.dot(a.astype(BF16), b.astype(BF16), preferred_element_type=F32)


def _dot_nt(a, b):
    return lax.dot_general(a.astype(BF16), b.astype(BF16), (((1,), (1,)), ((), ())),
                           preferred_element_type=F32)


def _dot_tn(a, b):
    return lax.dot_general(a.astype(BF16), b.astype(BF16), (((0,), (0,)), ((), ())),
                           preferred_element_type=F32)


def _silu(x):
    return x * jax.nn.sigmoid(x)


def _gated_rmsnorm(o, z, w):
    ms = jnp.mean(o * o, axis=-1, keepdims=True)
    return o * lax.rsqrt(ms + RMS_EPS) * w * _silu(z)


def _matmul_kernel(x_ref, w_ref, o_ref, *, w_transposed, sub_tiles):
    tn = o_ref.shape[1] // sub_tiles
    x = x_ref[...]
    for c in range(sub_tiles):
        cols = slice(c * tn, (c + 1) * tn)
        if w_transposed:
            o_ref[:, cols] = lax.dot_general(x, w_ref[cols, :].astype(BF16),
                                             (((1,), (1,)), ((), ())), preferred_element_type=F32)
        else:
            o_ref[:, cols] = jnp.dot(x, w_ref[:, cols].astype(BF16), preferred_element_type=F32)


def _matmul(x, w3, n, tm, tn, w_transposed=False, sub_tiles=2):
    m, k = x.shape
    sub_tiles = sub_tiles if n % (tn * sub_tiles) == 0 else 1
    tm, tn = min(tm, m), min(tn * sub_tiles, n)
    if w_transposed:
        w_spec = pl.BlockSpec((None, tn, k), lambda i, j: (0, j, 0))
    else:
        w_spec = pl.BlockSpec((None, k, tn), lambda i, j: (0, 0, j))
    return pl.pallas_call(
        functools.partial(_matmul_kernel, w_transposed=w_transposed, sub_tiles=sub_tiles),
        grid=(m // tm, n // tn),
        in_specs=[pl.BlockSpec((tm, k), lambda i, j: (i, 0), pipeline_mode=pl.Buffered(1)),
                  w_spec],
        out_specs=pl.BlockSpec((tm, tn), lambda i, j: (i, j)),
        out_shape=jax.ShapeDtypeStruct((m, n), F32),
        compiler_params=pltpu.CompilerParams(
            dimension_semantics=("parallel", "parallel"), vmem_limit_bytes=VMEM_LIMIT),
        name="in_proj",
    )(x, w3)


def _split_bf16(v):
    hi = v.astype(BF16)
    lo = (v - hi.astype(F32)).astype(BF16)
    return hi, lo


def _out_proj_kernel(y_ref, w_ref, x_ref, o_ref, *, alpha):
    o_ref[...] = alpha * x_ref[...] + jnp.dot(y_ref[...], w_ref[...].astype(BF16),
                                              preferred_element_type=F32)


def _out_proj(y, w3, x, alpha, tm, tn):
    m, k = y.shape
    n = w3.shape[2]
    tm, tn = min(tm, m), min(tn, n)
    return pl.pallas_call(
        functools.partial(_out_proj_kernel, alpha=alpha),
        grid=(m // tm, n // tn),
        in_specs=[pl.BlockSpec((tm, k), lambda i, j: (i, 0)),
                  pl.BlockSpec((None, k, tn), lambda i, j: (0, 0, j)),
                  pl.BlockSpec((tm, tn), lambda i, j: (i, j))],
        out_specs=pl.BlockSpec((tm, tn), lambda i, j: (i, j)),
        out_shape=jax.ShapeDtypeStruct((m, n), F32),
        compiler_params=pltpu.CompilerParams(
            dimension_semantics=("parallel", "parallel"), vmem_limit_bytes=VMEM_LIMIT),
        name="out_proj",
    )(y, w3, x)


def _layer_norm(h, g, b):
    mu = jnp.mean(h, axis=-1, keepdims=True)
    c = h - mu
    var = jnp.mean(c * c, axis=-1, keepdims=True)
    return c * lax.rsqrt(var + LN_EPS) * g + b


def _ln_kernel(h_ref, g_ref, b_ref, o_ref):
    o_ref[...] = _layer_norm(h_ref[...], g_ref[...], b_ref[...])


def _ln_gate_kernel(h_ref, g_ref, b_ref, w_ref, o_ref, obf_ref, ab_ref):
    out = _layer_norm(h_ref[...], g_ref[...], b_ref[...])
    o_ref[...] = out
    obf_ref[...] = out.astype(BF16)
    xh, xl = _split_bf16(out)
    wh, wl = _split_bf16(w_ref[...])
    dot = functools.partial(jnp.dot, preferred_element_type=F32)
    ab_ref[...] = dot(xh, wh) + (dot(xl, wh) + dot(xh, wl))


def _ln(h, g, b, tm):
    m, n = h.shape
    tm = min(tm, m)
    row = pl.BlockSpec((tm, n), lambda i: (i, 0))
    vec = pl.BlockSpec((1, n), lambda i: (0, 0))
    return pl.pallas_call(
        _ln_kernel, grid=(m // tm,), in_specs=[row, vec, vec], out_specs=row,
        out_shape=jax.ShapeDtypeStruct((m, n), F32),
        compiler_params=pltpu.CompilerParams(
            dimension_semantics=("parallel",), vmem_limit_bytes=VMEM_LIMIT),
        name="layer_norm",
    )(h, g, b)


def _ln_gate(h, g, b, w_gate, tm):
    m, n = h.shape
    ng = w_gate.shape[1]
    tm = min(tm, m)
    row = pl.BlockSpec((tm, n), lambda i: (i, 0))
    vec = pl.BlockSpec((1, n), lambda i: (0, 0))
    return pl.pallas_call(
        _ln_gate_kernel, grid=(m // tm,),
        in_specs=[row, vec, vec, pl.BlockSpec((n, ng), lambda i: (0, 0))],
        out_specs=[row, row, pl.BlockSpec((tm, ng), lambda i: (i, 0))],
        out_shape=[jax.ShapeDtypeStruct((m, n), F32), jax.ShapeDtypeStruct((m, n), BF16),
                   jax.ShapeDtypeStruct((m, ng), F32)],
        compiler_params=pltpu.CompilerParams(
            dimension_semantics=("parallel",), vmem_limit_bytes=VMEM_LIMIT),
        name="layer_norm_gate",
    )(h, g, b, w_gate)


def _row_of(ref, h, base, s):
    return ref[h, pl.ds(base + s, 1), :]


def _hgrn2_offdiag(q, k, cum2):
    offs = []
    for blk in range(1, CHUNK // SUB):
        r0 = blk * SUB
        start = cum2[r0 - 1:r0]
        q_s = q[r0:r0 + SUB] * jnp.exp2(cum2[r0:r0 + SUB] - start)
        k_s = jnp.concatenate([k[:r0] * jnp.exp2(start - cum2[:r0]),
                               jnp.zeros((CHUNK - r0, HEAD_DIM), F32)], axis=0)
        offs.append(_dot_nt(q_s, k_s))
    return offs


def _hgrn2_intra(q, cum2, offs, k_rows, cum_rows):
    lane = lax.broadcasted_iota(jnp.int32, (SUB, CHUNK), 1)
    row = lax.broadcasted_iota(jnp.int32, (SUB, CHUNK), 0)
    blocks = []
    for blk in range(CHUNK // SUB):
        r0 = blk * SUB
        q_b, cum_b = q[r0:r0 + SUB], cum2[r0:r0 + SUB]
        a = jnp.zeros((SUB, CHUNK), F32)
        for j in range(SUB):
            s = r0 + j
            diff = cum_b - cum_rows(s)
            if j > 0:
                diff = jnp.minimum(diff, 0.0)
            col = jnp.sum(q_b * (k_rows(s) * jnp.exp2(diff)), axis=-1, keepdims=True)
            a = jnp.where(lane == s, col, a)
        a = jnp.where(lane <= row + r0, a, 0.0)
        if blk > 0:
            a = jnp.where(lane < r0, offs[blk - 1], a)
        blocks.append(a)
    return jnp.concatenate(blocks, axis=0)


def _hgrn2_kernel(q_ref, f_ref, i_ref, z_ref, lbp_ref, nw_ref, tri_ref, y_ref, st_ref,
                  kbuf_ref, cbuf_ref, *, hb, t_blk, layer):
    @pl.when(pl.program_id(2) == 0)
    def _():
        st_ref[...] = jnp.zeros_like(st_ref)

    tri = tri_ref[...]
    nw = nw_ref[...]
    hrange = range(hb)
    q, k, v, cum2 = [], [], [], []
    for h in hrange:
        cols = slice(h * HEAD_DIM, (h + 1) * HEAD_DIM)
        lbp = lbp_ref[:, cols]
        ex = jnp.exp(lbp - jnp.max(lbp, axis=0, keepdims=True))
        lb = jnp.sum(ex[:layer + 1], axis=0, keepdims=True) / jnp.sum(ex, axis=0, keepdims=True)

        q.append(_silu(q_ref[:, cols]))
        forget = lb + (1.0 - lb) * jax.nn.sigmoid(f_ref[:, cols])
        k.append(1.0 - forget)
        v.append(i_ref[:, cols])

        g_hi, g_lo = _split_bf16(jnp.log2(forget))
        cs = jnp.dot(tri, jnp.concatenate([g_hi, g_lo], axis=1), preferred_element_type=F32)
        cum2.append(cs[:, :HEAD_DIM] + cs[:, HEAD_DIM:])
        kbuf_ref[h] = k[h]
        cbuf_ref[h] = cum2[h]

    st = [st_ref[h] for h in hrange]
    outs = [[] for _ in hrange]
    n_chunks = t_blk // CHUNK
    chunk_rows = [slice(c * CHUNK, (c + 1) * CHUNK) for c in range(n_chunks)]
    offs = {(c, h): _hgrn2_offdiag(q[h][chunk_rows[c]], k[h][chunk_rows[c]], cum2[h][chunk_rows[c]])
            for c in range(n_chunks) for h in hrange}
    for c in range(n_chunks):
        rows = chunk_rows[c]
        for h in hrange:
            q_c, k_c, v_c, cum_c = q[h][rows], k[h][rows], v[h][rows], cum2[h][rows]
            last = cum_c[CHUNK - 1:CHUNK]
            k_rows = functools.partial(_row_of, kbuf_ref, h, c * CHUNK)
            cum_rows = functools.partial(_row_of, cbuf_ref, h, c * CHUNK)
            a = _hgrn2_intra(q_c, cum_c, offs[c, h], k_rows, cum_rows)
            outs[h].append(_dot(a, v_c) + _dot_nt(q_c * jnp.exp2(cum_c), st[h]))
            st[h] = jnp.exp2(last) * st[h] + _dot_tn(v_c, k_c * jnp.exp2(last - cum_c))
    for h in hrange:
        cols = slice(h * HEAD_DIM, (h + 1) * HEAD_DIM)
        st_ref[h] = st[h]
        o_all = jnp.concatenate(outs[h], axis=0)
        y_ref[:, cols] = _gated_rmsnorm(o_all, z_ref[:, cols], nw).astype(y_ref.dtype)


def _block_diag_tri(t_blk):
    r = lax.broadcasted_iota(jnp.int32, (t_blk, t_blk), 0)
    c = lax.broadcasted_iota(jnp.int32, (t_blk, t_blk), 1)
    return ((c <= r) & (c // CHUNK == r // CHUNK)).astype(BF16)


def _hgrn2_scan(hproj, lb_params, norm_w, batch, seq, d, layer, hb, t_blk):
    hb = min(hb, d // HEAD_DIM)
    t_blk = min(t_blk, seq)
    w = hb * HEAD_DIM
    nhb = d // w
    nt = seq // t_blk

    def col_spec(part):
        return pl.BlockSpec((t_blk, w), lambda b, h, t: (b * nt + t, part * nhb + h))

    return pl.pallas_call(
        functools.partial(_hgrn2_kernel, hb=hb, t_blk=t_blk, layer=layer),
        grid=(batch, nhb, nt),
        in_specs=[col_spec(0), col_spec(1), col_spec(2), col_spec(3),
                  pl.BlockSpec((lb_params.shape[0], w), lambda b, h, t: (0, h)),
                  pl.BlockSpec((1, HEAD_DIM), lambda b, h, t: (0, 0)),
                  pl.BlockSpec((t_blk, t_blk), lambda b, h, t: (0, 0))],
        out_specs=pl.BlockSpec((t_blk, w), lambda b, h, t: (b * nt + t, h)),
        out_shape=jax.ShapeDtypeStruct((batch * seq, d), BF16),
        scratch_shapes=[pltpu.VMEM((hb, HEAD_DIM, HEAD_DIM), F32),
                        pltpu.VMEM((hb, t_blk, HEAD_DIM), F32),
                        pltpu.VMEM((hb, t_blk, HEAD_DIM), F32)],
        compiler_params=pltpu.CompilerParams(
            dimension_semantics=("parallel", "parallel", "arbitrary"),
            vmem_limit_bytes=VMEM_LIMIT),
        name="hgrn2_scan",
    )(hproj, hproj, hproj, hproj, lb_params, norm_w, _block_diag_tri(t_blk))


def _causal_conv_silu(buf_ref, cols, x, w, t_blk):
    buf_ref[pl.ds(SUBLANES, t_blk), cols] = x
    acc = x * w[CONV_TAPS - 1:CONV_TAPS]
    for j in range(1, CONV_TAPS):
        acc = acc + buf_ref[pl.ds(SUBLANES - j, t_blk), cols] * w[CONV_TAPS - 1 - j:CONV_TAPS - j]
    buf_ref[pl.ds(0, SUBLANES), cols] = x[t_blk - SUBLANES:]
    return _silu(acc)


def _l2_normalize(x):
    return x * lax.rsqrt(jnp.sum(x * x, axis=-1, keepdims=True) + L2_EPS)


def _gdn_kernel(alog_ref, dtb_ref, q_ref, k_ref, v_ref, z_ref, cwq_ref, cwk_ref, cwv_ref,
                ab_ref, nw_ref, y_ref, s_ref, stage_ref, *, hb, t_blk, heads):
    @pl.when(pl.program_id(2) == 0)
    def _():
        s_ref[...] = jnp.zeros_like(s_ref)
        stage_ref[:, pl.ds(0, SUBLANES), :] = jnp.zeros((3, SUBLANES, hb * HEAD_DIM), F32)

    nw = nw_ref[...]
    r = lax.broadcasted_iota(jnp.int32, (GROUP, GROUP), 0)
    c = lax.broadcasted_iota(jnp.int32, (GROUP, GROUP), 1)
    same_chunk = (r // CHUNK) == (c // CHUNK)
    eye = r == c
    incl = same_chunk & (c <= r)
    strict = same_chunk & (c < r)
    chunks_per_group = GROUP // CHUNK
    n_groups = t_blk // GROUP
    hrange = range(hb)
    problems = [(h, grp) for h in hrange for grp in range(n_groups)]

    q_all, k_all, v_all = [], [], []
    for h in hrange:
        cols = slice(h * HEAD_DIM, (h + 1) * HEAD_DIM)
        streams = []
        for idx, (x_ref, cw_ref) in enumerate(((q_ref, cwq_ref), (k_ref, cwk_ref), (v_ref, cwv_ref))):
            streams.append(_causal_conv_silu(stage_ref.at[idx], cols, x_ref[:, cols],
                                             cw_ref[:, cols], t_blk))
        q_all.append(_l2_normalize(streams[0]) * (HEAD_DIM ** -0.5))
        k_all.append(_l2_normalize(streams[1]))
        v_all.append(streams[2])

    def rows_of(grp):
        return slice(grp * GROUP, (grp + 1) * GROUP)

    kk = {p: _dot_nt(k_all[p[0]][rows_of(p[1])], k_all[p[0]][rows_of(p[1])]) for p in problems}
    qk = {p: _dot_nt(q_all[p[0]][rows_of(p[1])], k_all[p[0]][rows_of(p[1])]) for p in problems}

    cum_col, e_cum, rhs, inv, power = {}, {}, {}, {}, {}
    for p in problems:
        h, grp = p
        head = pl.program_id(1) * hb + h
        neg_a = -jnp.exp(jnp.full((1, GROUP), alog_ref[head], F32))
        beta_row = jax.nn.sigmoid(ab_ref[0, grp, pl.ds(head, 1), :])
        a_row = ab_ref[0, grp, pl.ds(heads + head, 1), :] + jnp.full((1, GROUP), dtb_ref[head], F32)
        g_row = neg_a * (jnp.maximum(a_row, 0.0) + jnp.log1p(jnp.exp(-jnp.abs(a_row))))

        cum_col[p] = jnp.sum(jnp.where(incl, g_row, 0.0), axis=1, keepdims=True)
        cum_row = jnp.sum(jnp.where(eye, cum_col[p], 0.0), axis=0, keepdims=True)
        beta_col = jnp.sum(jnp.where(eye, beta_row, 0.0), axis=1, keepdims=True)

        decay = jnp.exp(jnp.minimum(cum_col[p] - cum_row, 0.0))
        qk[p] = qk[p] * jnp.where(incl, decay, 0.0)
        power[p] = -(beta_col * kk[p]) * jnp.where(strict, decay, 0.0)
        inv[p] = jnp.where(eye, 1.0, 0.0) + power[p]
        e_cum[p] = jnp.exp(cum_col[p])
        rhs[p] = jnp.concatenate([beta_col * v_all[h][rows_of(grp)],
                                  (beta_col * e_cum[p]) * k_all[h][rows_of(grp)]], axis=1)

    for _ in range(5):
        for p in problems:
            power[p] = _dot(power[p], power[p])
        for p in problems:
            inv[p] = inv[p] + _dot(inv[p], power[p])
    sol = {p: _dot(inv[p], rhs[p]) for p in problems}

    state = [s_ref[h] for h in hrange]
    outs = [[] for _ in hrange]
    for grp in range(n_groups):
        u_parts = [[] for _ in hrange]
        for ci in range(chunks_per_group):
            cr = slice(ci * CHUNK, (ci + 1) * CHUNK)
            last_row = (ci + 1) * CHUNK - 1
            ws = []
            for h in hrange:
                p = (h, grp)
                q_dec = q_all[h][rows_of(grp)][cr] * e_cum[p][cr]
                ws.append(_dot(jnp.concatenate([sol[p][cr, HEAD_DIM:], q_dec], axis=0), state[h]))
            u = [sol[(h, grp)][cr, :HEAD_DIM] - ws[h][:CHUNK] for h in hrange]
            for h in hrange:
                p = (h, grp)
                last = cum_col[p][last_row:last_row + 1]
                k_dec = k_all[h][rows_of(grp)][cr] * jnp.exp(last - cum_col[p][cr])
                state[h] = jnp.exp(last) * state[h] + _dot_tn(k_dec, u[h])
            for h in hrange:
                p = (h, grp)
                u_parts[h].append(u[h])
                rest = [sol[p][(ci + 1) * CHUNK:, :HEAD_DIM]] if ci + 1 < chunks_per_group else []
                u_full = jnp.concatenate(u_parts[h] + rest, axis=0)
                outs[h].append(ws[h][CHUNK:] + _dot(qk[p][cr], u_full))
    for h in hrange:
        cols = slice(h * HEAD_DIM, (h + 1) * HEAD_DIM)
        s_ref[h] = state[h]
        o_all = jnp.concatenate(outs[h], axis=0)
        y_ref[:, cols] = _gated_rmsnorm(o_all, z_ref[:, cols], nw).astype(y_ref.dtype)


def _gdn_scan(hproj, conv_w, ab_rows, a_log, dt_bias, norm_w, batch, seq, d, hb, t_blk):
    heads = d // HEAD_DIM
    hb = min(hb, heads)
    t_blk = min(t_blk, seq)
    w = hb * HEAD_DIM
    nhb = d // w
    nt = seq // t_blk
    gpb = t_blk // GROUP

    def col_spec(part):
        return pl.BlockSpec((t_blk, w), lambda b, h, t: (b * nt + t, part * nhb + h))

    def conv_spec(part):
        return pl.BlockSpec((CONV_TAPS, w), lambda b, h, t: (0, part * nhb + h))

    smem = pl.BlockSpec(memory_space=pltpu.SMEM)
    return pl.pallas_call(
        functools.partial(_gdn_kernel, hb=hb, t_blk=t_blk, heads=heads),
        grid=(batch, nhb, nt),
        in_specs=[smem, smem, col_spec(0), col_spec(1), col_spec(2), col_spec(3),
                  conv_spec(0), conv_spec(1), conv_spec(2),
                  pl.BlockSpec((1, gpb, 2 * heads, GROUP), lambda b, h, t: (b, t, 0, 0)),
                  pl.BlockSpec((1, HEAD_DIM), lambda b, h, t: (0, 0))],
        out_specs=pl.BlockSpec((t_blk, w), lambda b, h, t: (b * nt + t, h)),
        out_shape=jax.ShapeDtypeStruct((batch * seq, d), BF16),
        scratch_shapes=[pltpu.VMEM((hb, HEAD_DIM, HEAD_DIM), F32),
                        pltpu.VMEM((3, SUBLANES + t_blk, w), F32)],
        compiler_params=pltpu.CompilerParams(
            dimension_semantics=("parallel", "parallel", "arbitrary"),
            vmem_limit_bytes=VMEM_LIMIT),
        name="gdn_scan",
    )(a_log, dt_bias, hproj, hproj, hproj, hproj, conv_w, conv_w, conv_w, ab_rows, norm_w)


def kernel(x, a_w_in, a_lower_bounds, a_norm_w, a_w_out, b_w_in, b_conv_w, b_a_log, b_dt_bias,
           b_norm_w, b_w_out, ln_g, ln_b):
    batch, seq, d = x.shape
    heads = d // HEAD_DIM
    depth = ln_g.shape[0]
    assert depth == 2 and a_w_in.shape[0] == 1 and b_w_in.shape[0] == 1
    assert a_w_in.shape[2] == 4 * d and b_w_in.shape[2] == 4 * d + 2 * heads
    assert seq % GROUP == 0 and d % HEAD_DIM == 0
    alpha = (2.0 * depth) ** 0.25
    m = batch * seq
    x2 = x.reshape(m, d)

    hproj = _matmul(x2.astype(BF16), a_w_in, 4 * d, tm=1024, tn=512)
    y = _hgrn2_scan(hproj, a_lower_bounds, a_norm_w, batch, seq, d, layer=0, hb=16, t_blk=256)
    h = _out_proj(y, a_w_out, x2, alpha, tm=1024, tn=512)
    b_w_in_t = jnp.swapaxes(b_w_in, 1, 2)
    w_gate = jnp.pad(b_w_in_t[0, 4 * d:, :].T, ((0, 0), (0, HEAD_DIM - 2 * heads)))
    x2, x2_bf, ab = _ln_gate(h, ln_g[0:1], ln_b[0:1], w_gate, tm=256)

    hproj = _matmul(x2_bf, b_w_in_t, 4 * d, tm=1024, tn=512, w_transposed=True)
    ab_rows = ab[:, :2 * heads].reshape(batch, seq // GROUP, GROUP, 2 * heads).transpose(0, 1, 3, 2)
    y = _gdn_scan(hproj, b_conv_w[0], ab_rows, b_a_log[0], b_dt_bias[0], b_norm_w,
                  batch, seq, d, hb=8, t_blk=256)
    h = _out_proj(y, b_w_out, x2, alpha, tm=1024, tn=512)
    return _ln(h, ln_g[1:2], ln_b[1:2], tm=256).reshape(batch, seq, d)
```

```python
import functools

import jax
import jax.numpy as jnp
from jax import lax
from jax.experimental import pallas as pl
from jax.experimental.pallas import tpu as pltpu

F32 = jnp.float32
BF16 = jnp.bfloat16

HEAD_DIM = 128
CHUNK = 64
SUB = 8
GROUP = 128
CONV_TAPS = 4
SUBLANES = 8
LN_EPS = 1e-5
RMS_EPS = 1e-6
L2_EPS = 1e-6
VMEM_LIMIT = 56 * 1024 * 1024


def _dot(a, b):
    return jnp.dot(a.astype(BF16), b.astype(BF16), preferred_element_type=F32)


def _dot_nt(a, b):
    return lax.dot_general(a.astype(BF16), b.astype(BF16), (((1,), (1,)), ((), ())),
                           preferred_element_type=F32)


def _dot_tn(a, b):
    return lax.dot_general(a.astype(BF16), b.astype(BF16), (((0,), (0,)), ((), ())),
                           preferred_element_type=F32)


def _silu(x):
    return x * jax.nn.sigmoid(x)


def _gated_rmsnorm(o, z, w):
    ms = jnp.mean(o * o, axis=-1, keepdims=True)
    return o * lax.rsqrt(ms + RMS_EPS) * w * _silu(z)


def _matmul_kernel(x_ref, w_ref, o_ref, *, w_transposed, sub_tiles):
    tn = o_ref.shape[1] // sub_tiles
    x = x_ref[...]
    for c in range(sub_tiles):
        cols = slice(c * tn, (c + 1) * tn)
        if w_transposed:
            o_ref[:, cols] = lax.dot_general(x, w_ref[cols, :].astype(BF16),
                                             (((1,), (1,)), ((), ())), preferred_element_type=F32)
        else:
            o_ref[:, cols] = jnp.dot(x, w_ref[:, cols].astype(BF16), preferred_element_type=F32)


def _matmul(x, w3, n, tm, tn, w_transposed=False, sub_tiles=2):
    m, k = x.shape
    sub_tiles = sub_tiles if n % (tn * sub_tiles) == 0 else 1
    tm, tn = min(tm, m), min(tn * sub_tiles, n)
    if w_transposed:
        w_spec = pl.BlockSpec((None, tn, k), lambda i, j: (0, j, 0))
    else:
        w_spec = pl.BlockSpec((None, k, tn), lambda i, j: (0, 0, j))
    return pl.pallas_call(
        functools.partial(_matmul_kernel, w_transposed=w_transposed, sub_tiles=sub_tiles),
        grid=(m // tm, n // tn),
        in_specs=[pl.BlockSpec((tm, k), lambda i, j: (i, 0), pipeline_mode=pl.Buffered(1)),
                  w_spec],
        out_specs=pl.BlockSpec((tm, tn), lambda i, j: (i, j)),
        out_shape=jax.ShapeDtypeStruct((m, n), F32),
        compiler_params=pltpu.CompilerParams(
            dimension_semantics=("parallel", "parallel"), vmem_limit_bytes=VMEM_LIMIT),
        name="in_proj",
    )(x, w3)


def _split_bf16(v):
    hi = v.astype(BF16)
    lo = (v - hi.astype(F32)).astype(BF16)
    return hi, lo


def _out_proj_kernel(y_ref, w_ref, x_ref, o_ref, *, alpha):
    o_ref[...] = alpha * x_ref[...] + jnp.dot(y_ref[...], w_ref[...].astype(BF16),
                                              preferred_element_type=F32)


def _out_proj(y, w3, x, alpha, tm, tn):
    m, k = y.shape
    n = w3.shape[2]
    tm, tn = min(tm, m), min(tn, n)
    return pl.pallas_call(
        functools.partial(_out_proj_kernel, alpha=alpha),
        grid=(m // tm, n // tn),
        in_specs=[pl.BlockSpec((tm, k), lambda i, j: (i, 0), pipeline_mode=pl.Buffered(1)),
                  pl.BlockSpec((None, k, tn), lambda i, j: (0, 0, j)),
                  pl.BlockSpec((tm, tn), lambda i, j: (i, j))],
        out_specs=pl.BlockSpec((tm, tn), lambda i, j: (i, j)),
        out_shape=jax.ShapeDtypeStruct((m, n), F32),
        compiler_params=pltpu.CompilerParams(
            dimension_semantics=("parallel", "parallel"), vmem_limit_bytes=VMEM_LIMIT),
        name="out_proj",
    )(y, w3, x)


def _layer_norm(h, g, b):
    mu = jnp.mean(h, axis=-1, keepdims=True)
    c = h - mu
    var = jnp.mean(c * c, axis=-1, keepdims=True)
    return c * lax.rsqrt(var + LN_EPS) * g + b


def _ln_kernel(h_ref, g_ref, b_ref, o_ref):
    o_ref[...] = _layer_norm(h_ref[...], g_ref[...], b_ref[...])


def _ln_gate_kernel(h_ref, g_ref, b_ref, w_ref, o_ref, obf_ref, ab_ref):
    out = _layer_norm(h_ref[...], g_ref[...], b_ref[...])
    o_ref[...] = out
    obf_ref[...] = out.astype(BF16)
    xh, xl = _split_bf16(out)
    wh, wl = _split_bf16(w_ref[...])
    dot = functools.partial(jnp.dot, preferred_element_type=F32)
    ab_ref[...] = dot(xh, wh) + (dot(xl, wh) + dot(xh, wl))


def _ln(h, g, b, tm):
    m, n = h.shape
    tm = min(tm, m)
    row = pl.BlockSpec((tm, n), lambda i: (i, 0))
    vec = pl.BlockSpec((1, n), lambda i: (0, 0))
    return pl.pallas_call(
        _ln_kernel, grid=(m // tm,), in_specs=[row, vec, vec], out_specs=row,
        out_shape=jax.ShapeDtypeStruct((m, n), F32),
        compiler_params=pltpu.CompilerParams(
            dimension_semantics=("parallel",), vmem_limit_bytes=VMEM_LIMIT),
        name="layer_norm",
    )(h, g, b)


def _ln_gate(h, g, b, w_gate, tm):
    m, n = h.shape
    ng = w_gate.shape[1]
    tm = min(tm, m)
    row = pl.BlockSpec((tm, n), lambda i: (i, 0))
    vec = pl.BlockSpec((1, n), lambda i: (0, 0))
    return pl.pallas_call(
        _ln_gate_kernel, grid=(m // tm,),
        in_specs=[row, vec, vec, pl.BlockSpec((n, ng), lambda i: (0, 0))],
        out_specs=[row, row, pl.BlockSpec((tm, ng), lambda i: (i, 0))],
        out_shape=[jax.ShapeDtypeStruct((m, n), F32), jax.ShapeDtypeStruct((m, n), BF16),
                   jax.ShapeDtypeStruct((m, ng), F32)],
        compiler_params=pltpu.CompilerParams(
            dimension_semantics=("parallel",), vmem_limit_bytes=VMEM_LIMIT),
        name="layer_norm_gate",
    )(h, g, b, w_gate)


def _row_of(ref, h, base, s):
    return ref[h, pl.ds(base + s, 1), :]


def _hgrn2_offdiag(q, k, cum2):
    offs = []
    for blk in range(1, CHUNK // SUB):
        r0 = blk * SUB
        start = cum2[r0 - 1:r0]
        q_s = q[r0:r0 + SUB] * jnp.exp2(cum2[r0:r0 + SUB] - start)
        k_s = jnp.concatenate([k[:r0] * jnp.exp2(start - cum2[:r0]),
                               jnp.zeros((CHUNK - r0, HEAD_DIM), F32)], axis=0)
        offs.append(_dot_nt(q_s, k_s))
    return offs


def _hgrn2_intra(q, cum2, offs, k_rows, cum_rows):
    lane = lax.broadcasted_iota(jnp.int32, (SUB, CHUNK), 1)
    row = lax.broadcasted_iota(jnp.int32, (SUB, CHUNK), 0)
    blocks = []
    for blk in range(CHUNK // SUB):
        r0 = blk * SUB
        q_b, cum_b = q[r0:r0 + SUB], cum2[r0:r0 + SUB]
        a = jnp.zeros((SUB, CHUNK), F32)
        for j in range(SUB):
            s = r0 + j
            diff = cum_b - cum_rows(s)
            if j > 0:
                diff = jnp.minimum(diff, 0.0)
            col = jnp.sum(q_b * (k_rows(s) * jnp.exp2(diff)), axis=-1, keepdims=True)
            a = jnp.where(lane == s, col, a)
        a = jnp.where(lane <= row + r0, a, 0.0)
        if blk > 0:
            a = jnp.where(lane < r0, offs[blk - 1], a)
        blocks.append(a)
    return jnp.concatenate(blocks, axis=0)


def _hgrn2_kernel(q_ref, f_ref, i_ref, z_ref, lbp_ref, nw_ref, tri_ref, y_ref, st_ref,
                  kbuf_ref, cbuf_ref, *, hb, t_blk, layer):
    @pl.when(pl.program_id(2) == 0)
    def _():
        st_ref[...] = jnp.zeros_like(st_ref)

    tri = tri_ref[...]
    nw = nw_ref[...]
    hrange = range(hb)
    q, k, v, cum2 = [], [], [], []
    for h in hrange:
        cols = slice(h * HEAD_DIM, (h + 1) * HEAD_DIM)
        lbp = lbp_ref[:, cols]
        ex = jnp.exp(lbp - jnp.max(lbp, axis=0, keepdims=True))
        lb = jnp.sum(ex[:layer + 1], axis=0, keepdims=True) / jnp.sum(ex, axis=0, keepdims=True)

        q.append(_silu(q_ref[:, cols]))
        forget = lb + (1.0 - lb) * jax.nn.sigmoid(f_ref[:, cols])
        k.append(1.0 - forget)
        v.append(i_ref[:, cols])

        g_hi, g_lo = _split_bf16(jnp.log2(forget))
        cs = jnp.dot(tri, jnp.concatenate([g_hi, g_lo], axis=1), preferred_element_type=F32)
        cum2.append(cs[:, :HEAD_DIM] + cs[:, HEAD_DIM:])
        kbuf_ref[h] = k[h]
        cbuf_ref[h] = cum2[h]

    st = [st_ref[h] for h in hrange]
    outs = [[] for _ in hrange]
    n_chunks = t_blk // CHUNK
    chunk_rows = [slice(c * CHUNK, (c + 1) * CHUNK) for c in range(n_chunks)]
    offs = {(c, h): _hgrn2_offdiag(q[h][chunk_rows[c]], k[h][chunk_rows[c]], cum2[h][chunk_rows[c]])
            for c in range(n_chunks) for h in hrange}
    for c in range(n_chunks):
        rows = chunk_rows[c]
        for h in hrange:
            q_c, k_c, v_c, cum_c = q[h][rows], k[h][rows], v[h][rows], cum2[h][rows]
            last = cum_c[CHUNK - 1:CHUNK]
            k_rows = functools.partial(_row_of, kbuf_ref, h, c * CHUNK)
            cum_rows = functools.partial(_row_of, cbuf_ref, h, c * CHUNK)
            a = _hgrn2_intra(q_c, cum_c, offs[c, h], k_rows, cum_rows)
            outs[h].append(_dot(a, v_c) + _dot_nt(q_c * jnp.exp2(cum_c), st[h]))
            st[h] = jnp.exp2(last) * st[h] + _dot_tn(v_c, k_c * jnp.exp2(last - cum_c))
    for h in hrange:
        cols = slice(h * HEAD_DIM, (h + 1) * HEAD_DIM)
        st_ref[h] = st[h]
        o_all = jnp.concatenate(outs[h], axis=0)
        y_ref[:, cols] = _gated_rmsnorm(o_all, z_ref[:, cols], nw).astype(y_ref.dtype)


def _block_diag_tri(t_blk):
    r = lax.broadcasted_iota(jnp.int32, (t_blk, t_blk), 0)
    c = lax.broadcasted_iota(jnp.int32, (t_blk, t_blk), 1)
    return ((c <= r) & (c // CHUNK == r // CHUNK)).astype(BF16)


def _hgrn2_scan(hproj, lb_params, norm_w, batch, seq, d, layer, hb, t_blk):
    hb = min(hb, d // HEAD_DIM)
    t_blk = min(t_blk, seq)
    w = hb * HEAD_DIM
    nhb = d // w
    nt = seq // t_blk

    def col_spec(part):
        return pl.BlockSpec((t_blk, w), lambda b, h, t: (b * nt + t, part * nhb + h))

    return pl.pallas_call(
        functools.partial(_hgrn2_kernel, hb=hb, t_blk=t_blk, layer=layer),
        grid=(batch, nhb, nt),
        in_specs=[col_spec(0), col_spec(1), col_spec(2), col_spec(3),
                  pl.BlockSpec((lb_params.shape[0], w), lambda b, h, t: (0, h)),
                  pl.BlockSpec((1, HEAD_DIM), lambda b, h, t: (0, 0)),
                  pl.BlockSpec((t_blk, t_blk), lambda b, h, t: (0, 0))],
        out_specs=pl.BlockSpec((t_blk, w), lambda b, h, t: (b * nt + t, h)),
        out_shape=jax.ShapeDtypeStruct((batch * seq, d), BF16),
        scratch_shapes=[pltpu.VMEM((hb, HEAD_DIM, HEAD_DIM), F32),
                        pltpu.VMEM((hb, t_blk, HEAD_DIM), F32),
                        pltpu.VMEM((hb, t_blk, HEAD_DIM), F32)],
        compiler_params=pltpu.CompilerParams(
            dimension_semantics=("parallel", "parallel", "arbitrary"),
            vmem_limit_bytes=VMEM_LIMIT),
        name="hgrn2_scan",
    )(hproj, hproj, hproj, hproj, lb_params, norm_w, _block_diag_tri(t_blk))


def _causal_conv_silu(buf_ref, cols, x, w, t_blk):
    buf_ref[pl.ds(SUBLANES, t_blk), cols] = x
    acc = x * w[CONV_TAPS - 1:CONV_TAPS]
    for j in range(1, CONV_TAPS):
        acc = acc + buf_ref[pl.ds(SUBLANES - j, t_blk), cols] * w[CONV_TAPS - 1 - j:CONV_TAPS - j]
    buf_ref[pl.ds(0, SUBLANES), cols] = x[t_blk - SUBLANES:]
    return _silu(acc)


def _l2_normalize(x):
    return x * lax.rsqrt(jnp.sum(x * x, axis=-1, keepdims=True) + L2_EPS)


def _gdn_kernel(alog_ref, dtb_ref, q_ref, k_ref, v_ref, z_ref, cwq_ref, cwk_ref, cwv_ref,
                ab_ref, nw_ref, y_ref, s_ref, stage_ref, *, hb, t_blk, heads):
    @pl.when(pl.program_id(2) == 0)
    def _():
        s_ref[...] = jnp.zeros_like(s_ref)
        stage_ref[:, pl.ds(0, SUBLANES), :] = jnp.zeros((3, SUBLANES, hb * HEAD_DIM), F32)

    nw = nw_ref[...]
    r = lax.broadcasted_iota(jnp.int32, (GROUP, GROUP), 0)
    c = lax.broadcasted_iota(jnp.int32, (GROUP, GROUP), 1)
    same_chunk = (r // CHUNK) == (c // CHUNK)
    eye = r == c
    incl = same_chunk & (c <= r)
    strict = same_chunk & (c < r)
    chunks_per_group = GROUP // CHUNK
    n_groups = t_blk // GROUP
    hrange = range(hb)
    problems = [(h, grp) for h in hrange for grp in range(n_groups)]

    q_all, k_all, v_all = [], [], []
    for h in hrange:
        cols = slice(h * HEAD_DIM, (h + 1) * HEAD_DIM)
        streams = []
        for idx, (x_ref, cw_ref) in enumerate(((q_ref, cwq_ref), (k_ref, cwk_ref), (v_ref, cwv_ref))):
            streams.append(_causal_conv_silu(stage_ref.at[idx], cols, x_ref[:, cols],
                                             cw_ref[:, cols], t_blk))
        q_all.append(_l2_normalize(streams[0]) * (HEAD_DIM ** -0.5))
        k_all.append(_l2_normalize(streams[1]))
        v_all.append(streams[2])

    def rows_of(grp):
        return slice(grp * GROUP, (grp + 1) * GROUP)

    kk = {p: _dot_nt(k_all[p[0]][rows_of(p[1])], k_all[p[0]][rows_of(p[1])]) for p in problems}
    qk = {p: _dot_nt(q_all[p[0]][rows_of(p[1])], k_all[p[0]][rows_of(p[1])]) for p in problems}

    cum_col, e_cum, rhs, inv, power = {}, {}, {}, {}, {}
    for p in problems:
        h, grp = p
        head = pl.program_id(1) * hb + h
        neg_a = -jnp.exp(jnp.full((1, GROUP), alog_ref[head], F32))
        beta_row = jax.nn.sigmoid(ab_ref[0, grp, pl.ds(head, 1), :])
        a_row = ab_ref[0, grp, pl.ds(heads + head, 1), :] + jnp.full((1, GROUP), dtb_ref[head], F32)
        g_row = neg_a * (jnp.maximum(a_row, 0.0) + jnp.log1p(jnp.exp(-jnp.abs(a_row))))

        cum_col[p] = jnp.sum(jnp.where(incl, g_row, 0.0), axis=1, keepdims=True)
        cum_row = jnp.sum(jnp.where(eye, cum_col[p], 0.0), axis=0, keepdims=True)
        beta_col = jnp.sum(jnp.where(eye, beta_row, 0.0), axis=1, keepdims=True)

        decay = jnp.exp(jnp.minimum(cum_col[p] - cum_row, 0.0))
        qk[p] = qk[p] * jnp.where(incl, decay, 0.0)
        power[p] = -(beta_col * kk[p]) * jnp.where(strict, decay, 0.0)
        inv[p] = jnp.where(eye, 1.0, 0.0) + power[p]
        e_cum[p] = jnp.exp(cum_col[p])
        rhs[p] = jnp.concatenate([beta_col * v_all[h][rows_of(grp)],
                                  (beta_col * e_cum[p]) * k_all[h][rows_of(grp)]], axis=1)

    for _ in range(5):
        for p in problems:
            power[p] = _dot(power[p], power[p])
        for p in problems:
            inv[p] = inv[p] + _dot(inv[p], power[p])
    sol = {p: _dot(inv[p], rhs[p]) for p in problems}

    state = [s_ref[h] for h in hrange]
    outs = [[] for _ in hrange]
    for grp in range(n_groups):
        u_parts = [[] for _ in hrange]
        for ci in range(chunks_per_group):
            cr = slice(ci * CHUNK, (ci + 1) * CHUNK)
            last_row = (ci + 1) * CHUNK - 1
            ws = []
            for h in hrange:
                p = (h, grp)
                q_dec = q_all[h][rows_of(grp)][cr] * e_cum[p][cr]
                ws.append(_dot(jnp.concatenate([sol[p][cr, HEAD_DIM:], q_dec], axis=0), state[h]))
            u = [sol[(h, grp)][cr, :HEAD_DIM] - ws[h][:CHUNK] for h in hrange]
            for h in hrange:
                p = (h, grp)
                last = cum_col[p][last_row:last_row + 1]
                k_dec = k_all[h][rows_of(grp)][cr] * jnp.exp(last - cum_col[p][cr])
                state[h] = jnp.exp(last) * state[h] + _dot_tn(k_dec, u[h])
            for h in hrange:
                p = (h, grp)
                u_parts[h].append(u[h])
                rest = [sol[p][(ci + 1) * CHUNK:, :HEAD_DIM]] if ci + 1 < chunks_per_group else []
                u_full = jnp.concatenate(u_parts[h] + rest, axis=0)
                outs[h].append(ws[h][CHUNK:] + _dot(qk[p][cr], u_full))
    for h in hrange:
        cols = slice(h * HEAD_DIM, (h + 1) * HEAD_DIM)
        s_ref[h] = state[h]
        o_all = jnp.concatenate(outs[h], axis=0)
        y_ref[:, cols] = _gated_rmsnorm(o_all, z_ref[:, cols], nw).astype(y_ref.dtype)


def _gdn_scan(hproj, conv_w, ab_rows, a_log, dt_bias, norm_w, batch, seq, d, hb, t_blk):
    heads = d // HEAD_DIM
    hb = min(hb, heads)
    t_blk = min(t_blk, seq)
    w = hb * HEAD_DIM
    nhb = d // w
    nt = seq // t_blk
    gpb = t_blk // GROUP

    def col_spec(part):
        return pl.BlockSpec((t_blk, w), lambda b, h, t: (b * nt + t, part * nhb + h))

    def conv_spec(part):
        return pl.BlockSpec((CONV_TAPS, w), lambda b, h, t: (0, part * nhb + h))

    smem = pl.BlockSpec(memory_space=pltpu.SMEM)
    return pl.pallas_call(
        functools.partial(_gdn_kernel, hb=hb, t_blk=t_blk, heads=heads),
        grid=(batch, nhb, nt),
        in_specs=[smem, smem, col_spec(0), col_spec(1), col_spec(2), col_spec(3),
                  conv_spec(0), conv_spec(1), conv_spec(2),
                  pl.BlockSpec((1, gpb, 2 * heads, GROUP), lambda b, h, t: (b, t, 0, 0)),
                  pl.BlockSpec((1, HEAD_DIM), lambda b, h, t: (0, 0))],
        out_specs=pl.BlockSpec((t_blk, w), lambda b, h, t: (b * nt + t, h)),
        out_shape=jax.ShapeDtypeStruct((batch * seq, d), BF16),
        scratch_shapes=[pltpu.VMEM((hb, HEAD_DIM, HEAD_DIM), F32),
                        pltpu.VMEM((3, SUBLANES + t_blk, w), F32)],
        compiler_params=pltpu.CompilerParams(
            dimension_semantics=("parallel", "parallel", "arbitrary"),
            vmem_limit_bytes=VMEM_LIMIT),
        name="gdn_scan",
    )(a_log, dt_bias, hproj, hproj, hproj, hproj, conv_w, conv_w, conv_w, ab_rows, norm_w)


def kernel(x, a_w_in, a_lower_bounds, a_norm_w, a_w_out, b_w_in, b_conv_w, b_a_log, b_dt_bias,
           b_norm_w, b_w_out, ln_g, ln_b):
    batch, seq, d = x.shape
    heads = d // HEAD_DIM
    depth = ln_g.shape[0]
    assert depth == 2 and a_w_in.shape[0] == 1 and b_w_in.shape[0] == 1
    assert a_w_in.shape[2] == 4 * d and b_w_in.shape[2] == 4 * d + 2 * heads
    assert seq % GROUP == 0 and d % HEAD_DIM == 0
    alpha = (2.0 * depth) ** 0.25
    m = batch * seq
    x2 = x.reshape(m, d)

    hproj = _matmul(x2.astype(BF16), a_w_in, 4 * d, tm=2048, tn=512, sub_tiles=1)
    y = _hgrn2_scan(hproj, a_lower_bounds, a_norm_w, batch, seq, d, layer=0, hb=16, t_blk=256)
    h = _out_proj(y, a_w_out, x2, alpha, tm=2048, tn=512)
    b_w_in_t = jnp.swapaxes(b_w_in, 1, 2)
    w_gate = jnp.pad(b_w_in_t[0, 4 * d:, :].T, ((0, 0), (0, HEAD_DIM - 2 * heads)))
    x2, x2_bf, ab = _ln_gate(h, ln_g[0:1], ln_b[0:1], w_gate, tm=256)

    hproj = _matmul(x2_bf, b_w_in_t, 4 * d, tm=2048, tn=512, w_transposed=True, sub_tiles=1)
    ab_rows = ab[:, :2 * heads].reshape(batch, seq // GROUP, GROUP, 2 * heads).transpose(0, 1, 3, 2)
    y = _gdn_scan(hproj, b_conv_w[0], ab_rows, b_a_log[0], b_dt_bias[0], b_norm_w,
                  batch, seq, d, hb=8, t_blk=256)
    h = _out_proj(y, b_w_out, x2, alpha, tm=2048, tn=512)
    return _ln(h, ln_g[1:2], ln_b[1:2], tm=256).reshape(batch, seq, d)
```

```python
import functools

import jax
import jax.numpy as jnp
from jax import lax
from jax.experimental import pallas as pl
from jax.experimental.pallas import tpu as pltpu

F32 = jnp.float32
BF16 = jnp.bfloat16

HEAD_DIM = 128
CHUNK = 64
SUB = 8
GROUP = 128
CONV_TAPS = 4
SUBLANES = 8
LN_EPS = 1e-5
RMS_EPS = 1e-6
L2_EPS = 1e-6
VMEM_LIMIT = 56 * 1024 * 1024


def _dot(a, b):
    return jnp.dot(a.astype(BF16), b.astype(BF16), preferred_element_type=F32)


def _dot_nt(a, b):
    return lax.dot_general(a.astype(BF16), b.astype(BF16), (((1,), (1,)), ((), ())),
                           preferred_element_type=F32)


def _dot_tn(a, b):
    return lax.dot_general(a.astype(BF16), b.astype(BF16), (((0,), (0,)), ((), ())),
                           preferred_element_type=F32)


def _silu(x):
    return x * jax.nn.sigmoid(x)


def _gated_rmsnorm(o, z, w):
    ms = jnp.mean(o * o, axis=-1, keepdims=True)
    return o * lax.rsqrt(ms + RMS_EPS) * w * _silu(z)


def _matmul_kernel(x_ref, w_ref, o_ref, *, w_transposed):
    w = w_ref[...].astype(BF16)
    if w_transposed:
        o_ref[...] = lax.dot_general(x_ref[...], w, (((1,), (1,)), ((), ())),
                                     preferred_element_type=F32)
    else:
        o_ref[...] = jnp.dot(x_ref[...], w, preferred_element_type=F32)


def _matmul(x, w3, n, tm, tn, w_transposed=False):
    m, k = x.shape
    tm, tn = min(tm, m), min(tn, n)
    if w_transposed:
        w_spec = pl.BlockSpec((None, tn, k), lambda i, j: (0, j, 0))
    else:
        w_spec = pl.BlockSpec((None, k, tn), lambda i, j: (0, 0, j))
    return pl.pallas_call(
        functools.partial(_matmul_kernel, w_transposed=w_transposed),
        grid=(m // tm, n // tn),
        in_specs=[pl.BlockSpec((tm, k), lambda i, j: (i, 0), pipeline_mode=pl.Buffered(1)),
                  w_spec],
        out_specs=pl.BlockSpec((tm, tn), lambda i, j: (i, j)),
        out_shape=jax.ShapeDtypeStruct((m, n), F32),
        compiler_params=pltpu.CompilerParams(
            dimension_semantics=("parallel", "parallel"), vmem_limit_bytes=VMEM_LIMIT),
        name="in_proj",
    )(x, w3)


def _split_bf16(v):
    hi = v.astype(BF16)
    lo = (v - hi.astype(F32)).astype(BF16)
    return hi, lo


def _out_proj_kernel(y_ref, w_ref, x_ref, o_ref, *, alpha):
    o_ref[...] = alpha * x_ref[...] + jnp.dot(y_ref[...], w_ref[...].astype(BF16),
                                              preferred_element_type=F32)


def _out_proj(y, w3, x, alpha, tm, tn):
    m, k = y.shape
    n = w3.shape[2]
    tm, tn = min(tm, m), min(tn, n)
    return pl.pallas_call(
        functools.partial(_out_proj_kernel, alpha=alpha),
        grid=(m // tm, n // tn),
        in_specs=[pl.BlockSpec((tm, k), lambda i, j: (i, 0), pipeline_mode=pl.Buffered(1)),
                  pl.BlockSpec((None, k, tn), lambda i, j: (0, 0, j)),
                  pl.BlockSpec((tm, tn), lambda i, j: (i, j))],
        out_specs=pl.BlockSpec((tm, tn), lambda i, j: (i, j)),
        out_shape=jax.ShapeDtypeStruct((m, n), F32),
        compiler_params=pltpu.CompilerParams(
            dimension_semantics=("parallel", "parallel"), vmem_limit_bytes=VMEM_LIMIT),
        name="out_proj",
    )(y, w3, x)


def _layer_norm(h, g, b):
    mu = jnp.mean(h, axis=-1, keepdims=True)
    c = h - mu
    var = jnp.mean(c * c, axis=-1, keepdims=True)
    return c * lax.rsqrt(var + LN_EPS) * g + b


def _ln_kernel(h_ref, g_ref, b_ref, o_ref):
    o_ref[...] = _layer_norm(h_ref[...], g_ref[...], b_ref[...])


def _ln_gate_kernel(h_ref, g_ref, b_ref, w_ref, o_ref, obf_ref, ab_ref):
    out = _layer_norm(h_ref[...], g_ref[...], b_ref[...])
    o_ref[...] = out
    obf_ref[...] = out.astype(BF16)
    xh, xl = _split_bf16(out)
    wh, wl = _split_bf16(w_ref[...])
    dot = functools.partial(jnp.dot, preferred_element_type=F32)
    ab_ref[...] = dot(xh, wh) + (dot(xl, wh) + dot(xh, wl))


def _ln(h, g, b, tm):
    m, n = h.shape
    tm = min(tm, m)
    row = pl.BlockSpec((tm, n), lambda i: (i, 0))
    vec = pl.BlockSpec((1, n), lambda i: (0, 0))
    return pl.pallas_call(
        _ln_kernel, grid=(m // tm,), in_specs=[row, vec, vec], out_specs=row,
        out_shape=jax.ShapeDtypeStruct((m, n), F32),
        compiler_params=pltpu.CompilerParams(
            dimension_semantics=("parallel",), vmem_limit_bytes=VMEM_LIMIT),
        name="layer_norm",
    )(h, g, b)


def _ln_gate(h, g, b, w_gate, tm):
    m, n = h.shape
    ng = w_gate.shape[1]
    tm = min(tm, m)
    row = pl.BlockSpec((tm, n), lambda i: (i, 0))
    vec = pl.BlockSpec((1, n), lambda i: (0, 0))
    return pl.pallas_call(
        _ln_gate_kernel, grid=(m // tm,),
        in_specs=[row, vec, vec, pl.BlockSpec((n, ng), lambda i: (0, 0))],
        out_specs=[row, row, pl.BlockSpec((tm, ng), lambda i: (i, 0))],
        out_shape=[jax.ShapeDtypeStruct((m, n), F32), jax.ShapeDtypeStruct((m, n), BF16),
                   jax.ShapeDtypeStruct((m, ng), F32)],
        compiler_params=pltpu.CompilerParams(
            dimension_semantics=("parallel",), vmem_limit_bytes=VMEM_LIMIT),
        name="layer_norm_gate",
    )(h, g, b, w_gate)


def _row_of(ref, h, base, s):
    return ref[h, pl.ds(base + s, 1), :]


def _hgrn2_offdiag(q, k, cum2):
    offs = []
    for blk in range(1, CHUNK // SUB):
        r0 = blk * SUB
        start = cum2[r0 - 1:r0]
        q_s = q[r0:r0 + SUB] * jnp.exp2(cum2[r0:r0 + SUB] - start)
        k_s = jnp.concatenate([k[:r0] * jnp.exp2(start - cum2[:r0]),
                               jnp.zeros((CHUNK - r0, HEAD_DIM), F32)], axis=0)
        offs.append(_dot_nt(q_s, k_s))
    return offs


def _hgrn2_intra(q, cum2, offs, k_rows, cum_rows):
    lane = lax.broadcasted_iota(jnp.int32, (SUB, CHUNK), 1)
    row = lax.broadcasted_iota(jnp.int32, (SUB, CHUNK), 0)
    blocks = []
    for blk in range(CHUNK // SUB):
        r0 = blk * SUB
        q_b, cum_b = q[r0:r0 + SUB], cum2[r0:r0 + SUB]
        a = jnp.zeros((SUB, CHUNK), F32)
        for j in range(SUB):
            s = r0 + j
            diff = cum_b - cum_rows(s)
            if j > 0:
                diff = jnp.minimum(diff, 0.0)
            col = jnp.sum(q_b * (k_rows(s) * jnp.exp2(diff)), axis=-1, keepdims=True)
            a = jnp.where(lane == s, col, a)
        a = jnp.where(lane <= row + r0, a, 0.0)
        if blk > 0:
            a = jnp.where(lane < r0, offs[blk - 1], a)
        blocks.append(a)
    return jnp.concatenate(blocks, axis=0)


def _hgrn2_kernel(q_ref, f_ref, i_ref, z_ref, lbp_ref, nw_ref, tri_ref, y_ref, st_ref,
                  kbuf_ref, cbuf_ref, *, hb, t_blk, layer):
    @pl.when(pl.program_id(2) == 0)
    def _():
        st_ref[...] = jnp.zeros_like(st_ref)

    tri = tri_ref[...]
    nw = nw_ref[...]
    hrange = range(hb)
    q, k, v, cum2 = [], [], [], []
    for h in hrange:
        cols = slice(h * HEAD_DIM, (h + 1) * HEAD_DIM)
        lbp = lbp_ref[:, cols]
        ex = jnp.exp(lbp - jnp.max(lbp, axis=0, keepdims=True))
        lb = jnp.sum(ex[:layer + 1], axis=0, keepdims=True) / jnp.sum(ex, axis=0, keepdims=True)

        q.append(_silu(q_ref[:, cols]))
        forget = lb + (1.0 - lb) * jax.nn.sigmoid(f_ref[:, cols])
        k.append(1.0 - forget)
        v.append(i_ref[:, cols])

        g_hi, g_lo = _split_bf16(jnp.log2(forget))
        cs = jnp.dot(tri, jnp.concatenate([g_hi, g_lo], axis=1), preferred_element_type=F32)
        cum2.append(cs[:, :HEAD_DIM] + cs[:, HEAD_DIM:])
        kbuf_ref[h] = k[h]
        cbuf_ref[h] = cum2[h]

    st = [st_ref[h] for h in hrange]
    outs = [[] for _ in hrange]
    n_chunks = t_blk // CHUNK
    chunk_rows = [slice(c * CHUNK, (c + 1) * CHUNK) for c in range(n_chunks)]
    offs = {(c, h): _hgrn2_offdiag(q[h][chunk_rows[c]], k[h][chunk_rows[c]], cum2[h][chunk_rows[c]])
            for c in range(n_chunks) for h in hrange}
    for c in range(n_chunks):
        rows = chunk_rows[c]
        for h in hrange:
            q_c, k_c, v_c, cum_c = q[h][rows], k[h][rows], v[h][rows], cum2[h][rows]
            last = cum_c[CHUNK - 1:CHUNK]
            k_rows = functools.partial(_row_of, kbuf_ref, h, c * CHUNK)
            cum_rows = functools.partial(_row_of, cbuf_ref, h, c * CHUNK)
            a = _hgrn2_intra(q_c, cum_c, offs[c, h], k_rows, cum_rows)
            outs[h].append(_dot(a, v_c) + _dot_nt(q_c * jnp.exp2(cum_c), st[h]))
            st[h] = jnp.exp2(last) * st[h] + _dot_tn(v_c, k_c * jnp.exp2(last - cum_c))
    for h in hrange:
        cols = slice(h * HEAD_DIM, (h + 1) * HEAD_DIM)
        st_ref[h] = st[h]
        o_all = jnp.concatenate(outs[h], axis=0)
        y_ref[:, cols] = _gated_rmsnorm(o_all, z_ref[:, cols], nw).astype(y_ref.dtype)


def _block_diag_tri(t_blk):
    r = lax.broadcasted_iota(jnp.int32, (t_blk, t_blk), 0)
    c = lax.broadcasted_iota(jnp.int32, (t_blk, t_blk), 1)
    return ((c <= r) & (c // CHUNK == r // CHUNK)).astype(BF16)


def _hgrn2_scan(hproj, lb_params, norm_w, batch, seq, d, layer, hb, t_blk):
    hb = min(hb, d // HEAD_DIM)
    t_blk = min(t_blk, seq)
    w = hb * HEAD_DIM
    nhb = d // w
    nt = seq // t_blk

    def col_spec(part):
        return pl.BlockSpec((t_blk, w), lambda b, h, t: (b * nt + t, part * nhb + h))

    return pl.pallas_call(
        functools.partial(_hgrn2_kernel, hb=hb, t_blk=t_blk, layer=layer),
        grid=(batch, nhb, nt),
        in_specs=[col_spec(0), col_spec(1), col_spec(2), col_spec(3),
                  pl.BlockSpec((lb_params.shape[0], w), lambda b, h, t: (0, h)),
                  pl.BlockSpec((1, HEAD_DIM), lambda b, h, t: (0, 0)),
                  pl.BlockSpec((t_blk, t_blk), lambda b, h, t: (0, 0))],
        out_specs=pl.BlockSpec((t_blk, w), lambda b, h, t: (b * nt + t, h)),
        out_shape=jax.ShapeDtypeStruct((batch * seq, d), BF16),
        scratch_shapes=[pltpu.VMEM((hb, HEAD_DIM, HEAD_DIM), F32),
                        pltpu.VMEM((hb, t_blk, HEAD_DIM), F32),
                        pltpu.VMEM((hb, t_blk, HEAD_DIM), F32)],
        compiler_params=pltpu.CompilerParams(
            dimension_semantics=("parallel", "parallel", "arbitrary"),
            vmem_limit_bytes=VMEM_LIMIT),
        name="hgrn2_scan",
    )(hproj, hproj, hproj, hproj, lb_params, norm_w, _block_diag_tri(t_blk))


def _causal_conv_silu(buf_ref, x, w, t_blk):
    buf_ref[pl.ds(SUBLANES, t_blk), :] = x
    acc = x * w[CONV_TAPS - 1:CONV_TAPS]
    for j in range(1, CONV_TAPS):
        acc = acc + buf_ref[pl.ds(SUBLANES - j, t_blk), :] * w[CONV_TAPS - 1 - j:CONV_TAPS - j]
    buf_ref[pl.ds(0, SUBLANES), :] = x[t_blk - SUBLANES:]
    return _silu(acc)


def _l2_normalize(x):
    return x * lax.rsqrt(jnp.sum(x * x, axis=-1, keepdims=True) + L2_EPS)


def _gdn_kernel(alog_ref, dtb_ref, q_ref, k_ref, v_ref, z_ref, cwq_ref, cwk_ref, cwv_ref,
                ab_ref, nw_ref, y_ref, s_ref, stage_ref, *, hb, t_blk, heads):
    @pl.when(pl.program_id(2) == 0)
    def _():
        s_ref[...] = jnp.zeros_like(s_ref)
        stage_ref[:, :, pl.ds(0, SUBLANES), :] = jnp.zeros((3, hb, SUBLANES, HEAD_DIM), F32)

    nw = nw_ref[...]
    r = lax.broadcasted_iota(jnp.int32, (GROUP, GROUP), 0)
    c = lax.broadcasted_iota(jnp.int32, (GROUP, GROUP), 1)
    same_chunk = (r // CHUNK) == (c // CHUNK)
    eye = r == c
    incl = same_chunk & (c <= r)
    strict = same_chunk & (c < r)
    chunks_per_group = GROUP // CHUNK
    n_groups = t_blk // GROUP
    hrange = range(hb)
    problems = [(h, grp) for h in hrange for grp in range(n_groups)]

    q_all, k_all, v_all = [], [], []
    for h in hrange:
        cols = slice(h * HEAD_DIM, (h + 1) * HEAD_DIM)
        streams = []
        for idx, (x_ref, cw_ref) in enumerate(((q_ref, cwq_ref), (k_ref, cwk_ref), (v_ref, cwv_ref))):
            streams.append(_causal_conv_silu(stage_ref.at[idx, h], x_ref[:, cols],
                                             cw_ref[:, cols], t_blk))
        q_all.append(_l2_normalize(streams[0]) * (HEAD_DIM ** -0.5))
        k_all.append(_l2_normalize(streams[1]))
        v_all.append(streams[2])

    def rows_of(grp):
        return slice(grp * GROUP, (grp + 1) * GROUP)

    kk = {p: _dot_nt(k_all[p[0]][rows_of(p[1])], k_all[p[0]][rows_of(p[1])]) for p in problems}
    qk = {p: _dot_nt(q_all[p[0]][rows_of(p[1])], k_all[p[0]][rows_of(p[1])]) for p in problems}

    cum_col, e_cum, rhs, inv, power = {}, {}, {}, {}, {}
    for p in problems:
        h, grp = p
        head = pl.program_id(1) * hb + h
        neg_a = -jnp.exp(jnp.full((1, GROUP), alog_ref[head], F32))
        beta_row = jax.nn.sigmoid(ab_ref[0, grp, pl.ds(head, 1), :])
        a_row = ab_ref[0, grp, pl.ds(heads + head, 1), :] + jnp.full((1, GROUP), dtb_ref[head], F32)
        g_row = neg_a * (jnp.maximum(a_row, 0.0) + jnp.log1p(jnp.exp(-jnp.abs(a_row))))

        cum_col[p] = jnp.sum(jnp.where(incl, g_row, 0.0), axis=1, keepdims=True)
        cum_row = jnp.sum(jnp.where(eye, cum_col[p], 0.0), axis=0, keepdims=True)
        beta_col = jnp.sum(jnp.where(eye, beta_row, 0.0), axis=1, keepdims=True)

        decay = jnp.exp(jnp.minimum(cum_col[p] - cum_row, 0.0))
        qk[p] = qk[p] * jnp.where(incl, decay, 0.0)
        power[p] = -(beta_col * kk[p]) * jnp.where(strict, decay, 0.0)
        inv[p] = jnp.where(eye, 1.0, 0.0) + power[p]
        e_cum[p] = jnp.exp(cum_col[p])
        rhs[p] = jnp.concatenate([beta_col * v_all[h][rows_of(grp)],
                                  (beta_col * e_cum[p]) * k_all[h][rows_of(grp)]], axis=1)

    for _ in range(5):
        for p in problems:
            power[p] = _dot(power[p], power[p])
        for p in problems:
            inv[p] = inv[p] + _dot(inv[p], power[p])
    sol = {p: _dot(inv[p], rhs[p]) for p in problems}

    state = [s_ref[h] for h in hrange]
    outs = [[] for _ in hrange]
    for grp in range(n_groups):
        u_parts = [[] for _ in hrange]
        for ci in range(chunks_per_group):
            cr = slice(ci * CHUNK, (ci + 1) * CHUNK)
            last_row = (ci + 1) * CHUNK - 1
            ws = []
            for h in hrange:
                p = (h, grp)
                q_dec = q_all[h][rows_of(grp)][cr] * e_cum[p][cr]
                ws.append(_dot(jnp.concatenate([sol[p][cr, HEAD_DIM:], q_dec], axis=0), state[h]))
            u = [sol[(h, grp)][cr, :HEAD_DIM] - ws[h][:CHUNK] for h in hrange]
            for h in hrange:
                p = (h, grp)
                last = cum_col[p][last_row:last_row + 1]
                k_dec = k_all[h][rows_of(grp)][cr] * jnp.exp(last - cum_col[p][cr])
                state[h] = jnp.exp(last) * state[h] + _dot_tn(k_dec, u[h])
            for h in hrange:
                p = (h, grp)
                u_parts[h].append(u[h])
                rest = [sol[p][(ci + 1) * CHUNK:, :HEAD_DIM]] if ci + 1 < chunks_per_group else []
                u_full = jnp.concatenate(u_parts[h] + rest, axis=0)
                outs[h].append(ws[h][CHUNK:] + _dot(qk[p][cr], u_full))
    for h in hrange:
        cols = slice(h * HEAD_DIM, (h + 1) * HEAD_DIM)
        s_ref[h] = state[h]
        o_all = jnp.concatenate(outs[h], axis=0)
        y_ref[:, cols] = _gated_rmsnorm(o_all, z_ref[:, cols], nw).astype(y_ref.dtype)


def _gdn_scan(hproj, conv_w, ab_rows, a_log, dt_bias, norm_w, batch, seq, d, hb, t_blk):
    heads = d // HEAD_DIM
    hb = min(hb, heads)
    t_blk = min(t_blk, seq)
    w = hb * HEAD_DIM
    nhb = d // w
    nt = seq // t_blk
    gpb = t_blk // GROUP

    def col_spec(part):
        return pl.BlockSpec((t_blk, w), lambda b, h, t: (b * nt + t, part * nhb + h))

    def conv_spec(part):
        return pl.BlockSpec((CONV_TAPS, w), lambda b, h, t: (0, part * nhb + h))

    smem = pl.BlockSpec(memory_space=pltpu.SMEM)
    return pl.pallas_call(
        functools.partial(_gdn_kernel, hb=hb, t_blk=t_blk, heads=heads),
        grid=(batch, nhb, nt),
        in_specs=[smem, smem, col_spec(0), col_spec(1), col_spec(2), col_spec(3),
                  conv_spec(0), conv_spec(1), conv_spec(2),
                  pl.BlockSpec((1, gpb, 2 * heads, GROUP), lambda b, h, t: (b, t, 0, 0)),
                  pl.BlockSpec((1, HEAD_DIM), lambda b, h, t: (0, 0))],
        out_specs=pl.BlockSpec((t_blk, w), lambda b, h, t: (b * nt + t, h)),
        out_shape=jax.ShapeDtypeStruct((batch * seq, d), BF16),
        scratch_shapes=[pltpu.VMEM((hb, HEAD_DIM, HEAD_DIM), F32),
                        pltpu.VMEM((3, hb, SUBLANES + t_blk, HEAD_DIM), F32)],
        compiler_params=pltpu.CompilerParams(
            dimension_semantics=("parallel", "parallel", "arbitrary"),
            vmem_limit_bytes=VMEM_LIMIT),
        name="gdn_scan",
    )(a_log, dt_bias, hproj, hproj, hproj, hproj, conv_w, conv_w, conv_w, ab_rows, norm_w)


def kernel(x, a_w_in, a_lower_bounds, a_norm_w, a_w_out, b_w_in, b_conv_w, b_a_log, b_dt_bias,
           b_norm_w, b_w_out, ln_g, ln_b):
    batch, seq, d = x.shape
    heads = d // HEAD_DIM
    depth = ln_g.shape[0]
    assert depth == 2 and a_w_in.shape[0] == 1 and b_w_in.shape[0] == 1
    assert a_w_in.shape[2] == 4 * d and b_w_in.shape[2] == 4 * d + 2 * heads
    assert seq % GROUP == 0 and d % HEAD_DIM == 0
    alpha = (2.0 * depth) ** 0.25
    m = batch * seq
    x2 = x.reshape(m, d)

    hproj = _matmul(x2.astype(BF16), a_w_in, 4 * d, tm=2048, tn=512)
    y = _hgrn2_scan(hproj, a_lower_bounds, a_norm_w, batch, seq, d, layer=0, hb=16, t_blk=256)
    h = _out_proj(y, a_w_out, x2, alpha, tm=2048, tn=512)
    b_w_in_t = jnp.swapaxes(b_w_in, 1, 2)
    w_gate = jnp.pad(b_w_in_t[0, 4 * d:, :].T, ((0, 0), (0, HEAD_DIM - 2 * heads)))
    x2, x2_bf, ab = _ln_gate(h, ln_g[0:1], ln_b[0:1], w_gate, tm=256)

    hproj = _matmul(x2_bf, b_w_in_t, 4 * d, tm=2048, tn=512, w_transposed=True)
    ab_rows = ab[:, :2 * heads].reshape(batch, seq // GROUP, GROUP, 2 * heads).transpose(0, 1, 3, 2)
    y = _gdn_scan(hproj, b_conv_w[0], ab_rows, b_a_log[0], b_dt_bias[0], b_norm_w,
                  batch, seq, d, hb=8, t_blk=256)
    h = _out_proj(y, b_w_out, x2, alpha, tm=2048, tn=512)
    return _ln(h, ln_g[1:2], ln_b[1:2], tm=256).reshape(batch, seq, d)
```

```python
import functools

import jax
import jax.numpy as jnp
from jax import lax
from jax.experimental import pallas as pl
from jax.experimental.pallas import tpu as pltpu

F32 = jnp.float32
BF16 = jnp.bfloat16

HEAD_DIM = 128
CHUNK = 64
SUB = 8
GROUP = 128
CONV_TAPS = 4
SUBLANES = 8
LN_EPS = 1e-5
RMS_EPS = 1e-6
L2_EPS = 1e-6
VMEM_LIMIT = 56 * 1024 * 1024


def _dot(a, b):
    return jnp.dot(a.astype(BF16), b.astype(BF16), preferred_element_type=F32)


def _dot_nt(a, b):
    return lax.dot_general(a.astype(BF16), b.astype(BF16), (((1,), (1,)), ((), ())),
                           preferred_element_type=F32)


def _dot_tn(a, b):
    return lax.dot_general(a.astype(BF16), b.astype(BF16), (((0,), (0,)), ((), ())),
                           preferred_element_type=F32)


def _silu(x):
    h = 0.5 * x
    return h + h * jnp.tanh(h)


def _gated_rmsnorm(o, z, w):
    ms = jnp.mean(o * o, axis=-1, keepdims=True)
    return o * lax.rsqrt(ms + RMS_EPS) * w * _silu(z)


def _matmul_kernel(x_ref, w_ref, o_ref, *, w_transposed):
    w = w_ref[...].astype(BF16)
    if w_transposed:
        o_ref[...] = lax.dot_general(x_ref[...], w, (((1,), (1,)), ((), ())),
                                     preferred_element_type=F32)
    else:
        o_ref[...] = jnp.dot(x_ref[...], w, preferred_element_type=F32)


def _matmul(x, w3, n, tm, tn, w_transposed=False):
    m, k = x.shape
    tm, tn = min(tm, m), min(tn, n)
    if w_transposed:
        w_spec = pl.BlockSpec((None, tn, k), lambda i, j: (0, j, 0))
    else:
        w_spec = pl.BlockSpec((None, k, tn), lambda i, j: (0, 0, j))
    return pl.pallas_call(
        functools.partial(_matmul_kernel, w_transposed=w_transposed),
        grid=(m // tm, n // tn),
        in_specs=[pl.BlockSpec((tm, k), lambda i, j: (i, 0), pipeline_mode=pl.Buffered(1)),
                  w_spec],
        out_specs=pl.BlockSpec((tm, tn), lambda i, j: (i, j)),
        out_shape=jax.ShapeDtypeStruct((m, n), F32),
        compiler_params=pltpu.CompilerParams(
            dimension_semantics=("parallel", "parallel"), vmem_limit_bytes=VMEM_LIMIT),
        name="in_proj",
    )(x, w3)


def _split_bf16(v):
    hi = v.astype(BF16)
    lo = (v - hi.astype(F32)).astype(BF16)
    return hi, lo


def _out_proj_kernel(y_ref, w_ref, x_ref, o_ref, *, alpha):
    o_ref[...] = alpha * x_ref[...] + jnp.dot(y_ref[...], w_ref[...].astype(BF16),
                                              preferred_element_type=F32)


def _out_proj(y, w3, x, alpha, tm, tn):
    m, k = y.shape
    n = w3.shape[2]
    tm, tn = min(tm, m), min(tn, n)
    return pl.pallas_call(
        functools.partial(_out_proj_kernel, alpha=alpha),
        grid=(m // tm, n // tn),
        in_specs=[pl.BlockSpec((tm, k), lambda i, j: (i, 0), pipeline_mode=pl.Buffered(1)),
                  pl.BlockSpec((None, k, tn), lambda i, j: (0, 0, j)),
                  pl.BlockSpec((tm, tn), lambda i, j: (i, j))],
        out_specs=pl.BlockSpec((tm, tn), lambda i, j: (i, j)),
        out_shape=jax.ShapeDtypeStruct((m, n), F32),
        compiler_params=pltpu.CompilerParams(
            dimension_semantics=("parallel", "parallel"), vmem_limit_bytes=VMEM_LIMIT),
        name="out_proj",
    )(y, w3, x)


def _layer_norm(h, g, b):
    mu = jnp.mean(h, axis=-1, keepdims=True)
    c = h - mu
    var = jnp.mean(c * c, axis=-1, keepdims=True)
    return c * lax.rsqrt(var + LN_EPS) * g + b


def _ln_kernel(h_ref, g_ref, b_ref, o_ref):
    o_ref[...] = _layer_norm(h_ref[...], g_ref[...], b_ref[...])


def _ln_gate_kernel(h_ref, g_ref, b_ref, w_ref, o_ref, obf_ref, ab_ref):
    out = _layer_norm(h_ref[...], g_ref[...], b_ref[...])
    o_ref[...] = out
    obf_ref[...] = out.astype(BF16)
    xh, xl = _split_bf16(out)
    wh, wl = _split_bf16(w_ref[...])
    dot = functools.partial(jnp.dot, preferred_element_type=F32)
    ab_ref[...] = dot(xh, wh) + (dot(xl, wh) + dot(xh, wl))


def _ln(h, g, b, tm):
    m, n = h.shape
    tm = min(tm, m)
    row = pl.BlockSpec((tm, n), lambda i: (i, 0))
    vec = pl.BlockSpec((1, n), lambda i: (0, 0))
    return pl.pallas_call(
        _ln_kernel, grid=(m // tm,), in_specs=[row, vec, vec], out_specs=row,
        out_shape=jax.ShapeDtypeStruct((m, n), F32),
        compiler_params=pltpu.CompilerParams(
            dimension_semantics=("parallel",), vmem_limit_bytes=VMEM_LIMIT),
        name="layer_norm",
    )(h, g, b)


def _ln_gate(h, g, b, w_gate, tm):
    m, n = h.shape
    ng = w_gate.shape[1]
    tm = min(tm, m)
    row = pl.BlockSpec((tm, n), lambda i: (i, 0))
    vec = pl.BlockSpec((1, n), lambda i: (0, 0))
    return pl.pallas_call(
        _ln_gate_kernel, grid=(m // tm,),
        in_specs=[row, vec, vec, pl.BlockSpec((n, ng), lambda i: (0, 0))],
        out_specs=[row, row, pl.BlockSpec((tm, ng), lambda i: (i, 0))],
        out_shape=[jax.ShapeDtypeStruct((m, n), F32), jax.ShapeDtypeStruct((m, n), BF16),
                   jax.ShapeDtypeStruct((m, ng), F32)],
        compiler_params=pltpu.CompilerParams(
            dimension_semantics=("parallel",), vmem_limit_bytes=VMEM_LIMIT),
        name="layer_norm_gate",
    )(h, g, b, w_gate)


def _row_of(ref, h, base, s):
    return ref[h, pl.ds(base + s, 1), :]


def _hgrn2_offdiag(q, k, cum2):
    offs = []
    for blk in range(1, CHUNK // SUB):
        r0 = blk * SUB
        start = cum2[r0 - 1:r0]
        q_s = q[r0:r0 + SUB] * jnp.exp2(cum2[r0:r0 + SUB] - start)
        k_s = jnp.concatenate([k[:r0] * jnp.exp2(start - cum2[:r0]),
                               jnp.zeros((CHUNK - r0, HEAD_DIM), F32)], axis=0)
        offs.append(_dot_nt(q_s, k_s))
    return offs


def _hgrn2_intra(q, cum2, offs, k_rows, cum_rows):
    lane = lax.broadcasted_iota(jnp.int32, (SUB, CHUNK), 1)
    row = lax.broadcasted_iota(jnp.int32, (SUB, CHUNK), 0)
    blocks = []
    for blk in range(CHUNK // SUB):
        r0 = blk * SUB
        q_b, cum_b = q[r0:r0 + SUB], cum2[r0:r0 + SUB]
        a = jnp.zeros((SUB, CHUNK), F32)
        for j in range(SUB):
            s = r0 + j
            col = jnp.sum(q_b * (k_rows(s) * jnp.exp2(cum_b - cum_rows(s))), axis=-1, keepdims=True)
            a = jnp.where(lane == s, col, a)
        a = jnp.where(lane <= row + r0, a, 0.0)
        if blk > 0:
            a = jnp.where(lane < r0, offs[blk - 1], a)
        blocks.append(a)
    return jnp.concatenate(blocks, axis=0)


def _hgrn2_kernel(q_ref, f_ref, i_ref, z_ref, lbp_ref, nw_ref, tri_ref, y_ref, st_ref,
                  kbuf_ref, cbuf_ref, *, hb, t_blk, layer):
    @pl.when(pl.program_id(2) == 0)
    def _():
        st_ref[...] = jnp.zeros_like(st_ref)

    tri = tri_ref[...]
    nw = nw_ref[...]
    hrange = range(hb)
    q, k, v, cum2 = [], [], [], []
    for h in hrange:
        cols = slice(h * HEAD_DIM, (h + 1) * HEAD_DIM)
        lbp = lbp_ref[:, cols]
        ex = jnp.exp(lbp - jnp.max(lbp, axis=0, keepdims=True))
        lb = jnp.sum(ex[:layer + 1], axis=0, keepdims=True) / jnp.sum(ex, axis=0, keepdims=True)

        q.append(_silu(q_ref[:, cols]))
        forget = lb + (1.0 - lb) * jax.nn.sigmoid(f_ref[:, cols])
        k.append(1.0 - forget)
        v.append(i_ref[:, cols])

        g_hi, g_lo = _split_bf16(jnp.log2(forget))
        cs = jnp.dot(tri, jnp.concatenate([g_hi, g_lo], axis=1), preferred_element_type=F32)
        cum2.append(cs[:, :HEAD_DIM] + cs[:, HEAD_DIM:])
        kbuf_ref[h] = k[h]
        cbuf_ref[h] = cum2[h]

    st = [st_ref[h] for h in hrange]
    outs = [[] for _ in hrange]
    n_chunks = t_blk // CHUNK
    chunk_rows = [slice(c * CHUNK, (c + 1) * CHUNK) for c in range(n_chunks)]
    offs = {(c, h): _hgrn2_offdiag(q[h][chunk_rows[c]], k[h][chunk_rows[c]], cum2[h][chunk_rows[c]])
            for c in range(n_chunks) for h in hrange}
    for c in range(n_chunks):
        rows = chunk_rows[c]
        for h in hrange:
            q_c, k_c, v_c, cum_c = q[h][rows], k[h][rows], v[h][rows], cum2[h][rows]
            last = cum_c[CHUNK - 1:CHUNK]
            k_rows = functools.partial(_row_of, kbuf_ref, h, c * CHUNK)
            cum_rows = functools.partial(_row_of, cbuf_ref, h, c * CHUNK)
            a = _hgrn2_intra(q_c, cum_c, offs[c, h], k_rows, cum_rows)
            outs[h].append(_dot(a, v_c) + _dot_nt(q_c * jnp.exp2(cum_c), st[h]))
            st[h] = jnp.exp2(last) * st[h] + _dot_tn(v_c, k_c * jnp.exp2(last - cum_c))
    for h in hrange:
        cols = slice(h * HEAD_DIM, (h + 1) * HEAD_DIM)
        st_ref[h] = st[h]
        o_all = jnp.concatenate(outs[h], axis=0)
        y_ref[:, cols] = _gated_rmsnorm(o_all, z_ref[:, cols], nw).astype(y_ref.dtype)


def _block_diag_tri(t_blk):
    r = lax.broadcasted_iota(jnp.int32, (t_blk, t_blk), 0)
    c = lax.broadcasted_iota(jnp.int32, (t_blk, t_blk), 1)
    return ((c <= r) & (c // CHUNK == r // CHUNK)).astype(BF16)


def _hgrn2_scan(hproj, lb_params, norm_w, batch, seq, d, layer, hb, t_blk):
    hb = min(hb, d // HEAD_DIM)
    t_blk = min(t_blk, seq)
    w = hb * HEAD_DIM
    nhb = d // w
    nt = seq // t_blk

    def col_spec(part):
        return pl.BlockSpec((t_blk, w), lambda b, h, t: (b * nt + t, part * nhb + h))

    return pl.pallas_call(
        functools.partial(_hgrn2_kernel, hb=hb, t_blk=t_blk, layer=layer),
        grid=(batch, nhb, nt),
        in_specs=[col_spec(0), col_spec(1), col_spec(2), col_spec(3),
                  pl.BlockSpec((lb_params.shape[0], w), lambda b, h, t: (0, h)),
                  pl.BlockSpec((1, HEAD_DIM), lambda b, h, t: (0, 0)),
                  pl.BlockSpec((t_blk, t_blk), lambda b, h, t: (0, 0))],
        out_specs=pl.BlockSpec((t_blk, w), lambda b, h, t: (b * nt + t, h)),
        out_shape=jax.ShapeDtypeStruct((batch * seq, d), BF16),
        scratch_shapes=[pltpu.VMEM((hb, HEAD_DIM, HEAD_DIM), F32),
                        pltpu.VMEM((hb, t_blk, HEAD_DIM), F32),
                        pltpu.VMEM((hb, t_blk, HEAD_DIM), F32)],
        compiler_params=pltpu.CompilerParams(
            dimension_semantics=("parallel", "parallel", "arbitrary"),
            vmem_limit_bytes=VMEM_LIMIT),
        name="hgrn2_scan",
    )(hproj, hproj, hproj, hproj, lb_params, norm_w, _block_diag_tri(t_blk))


def _causal_conv_silu(buf_ref, x, w, t_blk):
    buf_ref[pl.ds(SUBLANES, t_blk), :] = x
    acc = x * w[CONV_TAPS - 1:CONV_TAPS]
    for j in range(1, CONV_TAPS):
        acc = acc + buf_ref[pl.ds(SUBLANES - j, t_blk), :] * w[CONV_TAPS - 1 - j:CONV_TAPS - j]
    buf_ref[pl.ds(0, SUBLANES), :] = x[t_blk - SUBLANES:]
    return _silu(acc)


def _l2_normalize(x):
    return x * lax.rsqrt(jnp.sum(x * x, axis=-1, keepdims=True) + L2_EPS)


def _gdn_kernel(alog_ref, dtb_ref, q_ref, k_ref, v_ref, z_ref, cwq_ref, cwk_ref, cwv_ref,
                ab_ref, nw_ref, y_ref, s_ref, stage_ref, *, hb, t_blk, heads):
    @pl.when(pl.program_id(2) == 0)
    def _():
        s_ref[...] = jnp.zeros_like(s_ref)
        stage_ref[:, :, pl.ds(0, SUBLANES), :] = jnp.zeros((3, hb, SUBLANES, HEAD_DIM), F32)

    nw = nw_ref[...]
    r = lax.broadcasted_iota(jnp.int32, (GROUP, GROUP), 0)
    c = lax.broadcasted_iota(jnp.int32, (GROUP, GROUP), 1)
    same_chunk = (r // CHUNK) == (c // CHUNK)
    eye = r == c
    incl = same_chunk & (c <= r)
    strict = same_chunk & (c < r)
    chunks_per_group = GROUP // CHUNK
    n_groups = t_blk // GROUP
    hrange = range(hb)
    problems = [(h, grp) for h in hrange for grp in range(n_groups)]

    q_all, k_all, v_all = [], [], []
    for h in hrange:
        cols = slice(h * HEAD_DIM, (h + 1) * HEAD_DIM)
        streams = []
        for idx, (x_ref, cw_ref) in enumerate(((q_ref, cwq_ref), (k_ref, cwk_ref), (v_ref, cwv_ref))):
            streams.append(_causal_conv_silu(stage_ref.at[idx, h], x_ref[:, cols],
                                             cw_ref[:, cols], t_blk))
        q_all.append(_l2_normalize(streams[0]) * (HEAD_DIM ** -0.5))
        k_all.append(_l2_normalize(streams[1]))
        v_all.append(streams[2])

    def rows_of(grp):
        return slice(grp * GROUP, (grp + 1) * GROUP)

    kk = {p: _dot_nt(k_all[p[0]][rows_of(p[1])], k_all[p[0]][rows_of(p[1])]) for p in problems}
    qk = {p: _dot_nt(q_all[p[0]][rows_of(p[1])], k_all[p[0]][rows_of(p[1])]) for p in problems}

    cum_col, e_cum, rhs, inv, power = {}, {}, {}, {}, {}
    for p in problems:
        h, grp = p
        head = pl.program_id(1) * hb + h
        neg_a = -jnp.exp(jnp.full((1, GROUP), alog_ref[head], F32))
        beta_row = jax.nn.sigmoid(ab_ref[0, grp, pl.ds(head, 1), :])
        a_row = ab_ref[0, grp, pl.ds(heads + head, 1), :] + jnp.full((1, GROUP), dtb_ref[head], F32)
        g_row = neg_a * (jnp.maximum(a_row, 0.0) + jnp.log1p(jnp.exp(-jnp.abs(a_row))))

        cum_col[p] = jnp.sum(jnp.where(incl, g_row, 0.0), axis=1, keepdims=True)
        cum_row = jnp.sum(jnp.where(eye, cum_col[p], 0.0), axis=0, keepdims=True)
        beta_col = jnp.sum(jnp.where(eye, beta_row, 0.0), axis=1, keepdims=True)

        decay = jnp.exp(jnp.minimum(cum_col[p] - cum_row, 0.0))
        qk[p] = qk[p] * jnp.where(incl, decay, 0.0)
        power[p] = -(beta_col * kk[p]) * jnp.where(strict, decay, 0.0)
        inv[p] = jnp.where(eye, 1.0, 0.0) + power[p]
        e_cum[p] = jnp.exp(cum_col[p])
        rhs[p] = jnp.concatenate([beta_col * v_all[h][rows_of(grp)],
                                  (beta_col * e_cum[p]) * k_all[h][rows_of(grp)]], axis=1)

    for _ in range(5):
        for p in problems:
            power[p] = _dot(power[p], power[p])
        for p in problems:
            inv[p] = inv[p] + _dot(inv[p], power[p])
    sol = {p: _dot(inv[p], rhs[p]) for p in problems}

    state = [s_ref[h] for h in hrange]
    outs = [[] for _ in hrange]
    for grp in range(n_groups):
        u_parts = [[] for _ in hrange]
        for ci in range(chunks_per_group):
            cr = slice(ci * CHUNK, (ci + 1) * CHUNK)
            last_row = (ci + 1) * CHUNK - 1
            ws = []
            for h in hrange:
                p = (h, grp)
                q_dec = q_all[h][rows_of(grp)][cr] * e_cum[p][cr]
                ws.append(_dot(jnp.concatenate([sol[p][cr, HEAD_DIM:], q_dec], axis=0), state[h]))
            u = [sol[(h, grp)][cr, :HEAD_DIM] - ws[h][:CHUNK] for h in hrange]
            for h in hrange:
                p = (h, grp)
                last = cum_col[p][last_row:last_row + 1]
                k_dec = k_all[h][rows_of(grp)][cr] * jnp.exp(last - cum_col[p][cr])
                state[h] = jnp.exp(last) * state[h] + _dot_tn(k_dec, u[h])
            for h in hrange:
                p = (h, grp)
                u_parts[h].append(u[h])
                rest = [sol[p][(ci + 1) * CHUNK:, :HEAD_DIM]] if ci + 1 < chunks_per_group else []
                u_full = jnp.concatenate(u_parts[h] + rest, axis=0)
                outs[h].append(ws[h][CHUNK:] + _dot(qk[p][cr], u_full))
    for h in hrange:
        cols = slice(h * HEAD_DIM, (h + 1) * HEAD_DIM)
        s_ref[h] = state[h]
        o_all = jnp.concatenate(outs[h], axis=0)
        y_ref[:, cols] = _gated_rmsnorm(o_all, z_ref[:, cols], nw).astype(y_ref.dtype)


def _gdn_scan(hproj, conv_w, ab_rows, a_log, dt_bias, norm_w, batch, seq, d, hb, t_blk):
    heads = d // HEAD_DIM
    hb = min(hb, heads)
    t_blk = min(t_blk, seq)
    w = hb * HEAD_DIM
    nhb = d // w
    nt = seq // t_blk
    gpb = t_blk // GROUP

    def col_spec(part):
        return pl.BlockSpec((t_blk, w), lambda b, h, t: (b * nt + t, part * nhb + h))

    def conv_spec(part):
        return pl.BlockSpec((CONV_TAPS, w), lambda b, h, t: (0, part * nhb + h))

    smem = pl.BlockSpec(memory_space=pltpu.SMEM)
    return pl.pallas_call(
        functools.partial(_gdn_kernel, hb=hb, t_blk=t_blk, heads=heads),
        grid=(batch, nhb, nt),
        in_specs=[smem, smem, col_spec(0), col_spec(1), col_spec(2), col_spec(3),
                  conv_spec(0), conv_spec(1), conv_spec(2),
                  pl.BlockSpec((1, gpb, 2 * heads, GROUP), lambda b, h, t: (b, t, 0, 0)),
                  pl.BlockSpec((1, HEAD_DIM), lambda b, h, t: (0, 0))],
        out_specs=pl.BlockSpec((t_blk, w), lambda b, h, t: (b * nt + t, h)),
        out_shape=jax.ShapeDtypeStruct((batch * seq, d), BF16),
        scratch_shapes=[pltpu.VMEM((hb, HEAD_DIM, HEAD_DIM), F32),
                        pltpu.VMEM((3, hb, SUBLANES + t_blk, HEAD_DIM), F32)],
        compiler_params=pltpu.CompilerParams(
            dimension_semantics=("parallel", "parallel", "arbitrary"),
            vmem_limit_bytes=VMEM_LIMIT),
        name="gdn_scan",
    )(a_log, dt_bias, hproj, hproj, hproj, hproj, conv_w, conv_w, conv_w, ab_rows, norm_w)


def kernel(x, a_w_in, a_lower_bounds, a_norm_w, a_w_out, b_w_in, b_conv_w, b_a_log, b_dt_bias,
           b_norm_w, b_w_out, ln_g, ln_b):
    batch, seq, d = x.shape
    heads = d // HEAD_DIM
    depth = ln_g.shape[0]
    assert depth == 2 and a_w_in.shape[0] == 1 and b_w_in.shape[0] == 1
    assert a_w_in.shape[2] == 4 * d and b_w_in.shape[2] == 4 * d + 2 * heads
    assert seq % GROUP == 0 and d % HEAD_DIM == 0
    alpha = (2.0 * depth) ** 0.25
    m = batch * seq
    x2 = x.reshape(m, d)

    hproj = _matmul(x2.astype(BF16), a_w_in, 4 * d, tm=2048, tn=512)
    y = _hgrn2_scan(hproj, a_lower_bounds, a_norm_w, batch, seq, d, layer=0, hb=16, t_blk=256)
    h = _out_proj(y, a_w_out, x2, alpha, tm=2048, tn=512)
    b_w_in_t = jnp.swapaxes(b_w_in, 1, 2)
    w_gate = jnp.pad(b_w_in_t[0, 4 * d:, :].T, ((0, 0), (0, HEAD_DIM - 2 * heads)))
    x2, x2_bf, ab = _ln_gate(h, ln_g[0:1], ln_b[0:1], w_gate, tm=256)

    hproj = _matmul(x2_bf, b_w_in_t, 4 * d, tm=2048, tn=512, w_transposed=True)
    ab_rows = ab[:, :2 * heads].reshape(batch, seq // GROUP, GROUP, 2 * heads).transpose(0, 1, 3, 2)
    y = _gdn_scan(hproj, b_conv_w[0], ab_rows, b_a_log[0], b_dt_bias[0], b_norm_w,
                  batch, seq, d, hb=8, t_blk=256)
    h = _out_proj(y, b_w_out, x2, alpha, tm=2048, tn=512)
    return _ln(h, ln_g[1:2], ln_b[1:2], tm=256).reshape(batch, seq, d)
```

```python
import functools

import jax
import jax.numpy as jnp
from jax import lax
from jax.experimental import pallas as pl
from jax.experimental.pallas import tpu as pltpu

F32 = jnp.float32
BF16 = jnp.bfloat16

HEAD_DIM = 128
CHUNK = 64
SUB = 8
GROUP = 128
CONV_TAPS = 4
SUBLANES = 8
LN_EPS = 1e-5
RMS_EPS = 1e-6
L2_EPS = 1e-6
VMEM_LIMIT = 56 * 1024 * 1024


def _dot(a, b):
    return jnp.dot(a.astype(BF16), b.astype(BF16), preferred_element_type=F32)


def _dot_nt(a, b):
    return lax.dot_general(a.astype(BF16), b.astype(BF16), (((1,), (1,)), ((), ())),
                           preferred_element_type=F32)


def _dot_tn(a, b):
    return lax.dot_general(a.astype(BF16), b.astype(BF16), (((0,), (0,)), ((), ())),
                           preferred_element_type=F32)


def _silu(x):
    h = 0.5 * x
    return h + h * jnp.tanh(h)


def _gated_rmsnorm(o, z, w):
    ms = jnp.mean(o * o, axis=-1, keepdims=True)
    return o * lax.rsqrt(ms + RMS_EPS) * w * _silu(z)


def _matmul_kernel(x_ref, w_ref, o_ref, *, w_transposed):
    w = w_ref[...].astype(BF16)
    if w_transposed:
        o_ref[...] = lax.dot_general(x_ref[...], w, (((1,), (1,)), ((), ())),
                                     preferred_element_type=F32)
    else:
        o_ref[...] = jnp.dot(x_ref[...], w, preferred_element_type=F32)


def _matmul(x, w3, n, tm, tn, w_transposed=False):
    m, k = x.shape
    tm, tn = min(tm, m), min(tn, n)
    if w_transposed:
        w_spec = pl.BlockSpec((None, tn, k), lambda i, j: (0, j, 0))
    else:
        w_spec = pl.BlockSpec((None, k, tn), lambda i, j: (0, 0, j))
    return pl.pallas_call(
        functools.partial(_matmul_kernel, w_transposed=w_transposed),
        grid=(m // tm, n // tn),
        in_specs=[pl.BlockSpec((tm, k), lambda i, j: (i, 0), pipeline_mode=pl.Buffered(1)),
                  w_spec],
        out_specs=pl.BlockSpec((tm, tn), lambda i, j: (i, j)),
        out_shape=jax.ShapeDtypeStruct((m, n), F32),
        compiler_params=pltpu.CompilerParams(
            dimension_semantics=("parallel", "parallel"), vmem_limit_bytes=VMEM_LIMIT),
        name="in_proj",
    )(x, w3)


def _split_bf16(v):
    hi = v.astype(BF16)
    lo = (v - hi.astype(F32)).astype(BF16)
    return hi, lo


def _out_proj_kernel(y_ref, w_ref, x_ref, o_ref, *, alpha):
    o_ref[...] = alpha * x_ref[...] + jnp.dot(y_ref[...], w_ref[...].astype(BF16),
                                              preferred_element_type=F32)


def _out_proj_ln_res_kernel(y_ref, w_ref, h_ref, mu_ref, rstd_ref, g_ref, b_ref, o_ref, *, alpha):
    x = (h_ref[...] - mu_ref[...]) * rstd_ref[...] * g_ref[...] + b_ref[...]
    o_ref[...] = alpha * x + jnp.dot(y_ref[...], w_ref[...].astype(BF16),
                                     preferred_element_type=F32)


def _out_proj(y, w3, x, alpha, tm, tn, ln_stats=None):
    m, k = y.shape
    n = w3.shape[2]
    tm, tn = min(tm, m), min(tn, n)
    in_specs = [pl.BlockSpec((tm, k), lambda i, j: (i, 0), pipeline_mode=pl.Buffered(1)),
                pl.BlockSpec((None, k, tn), lambda i, j: (0, 0, j)),
                pl.BlockSpec((tm, tn), lambda i, j: (i, j))]
    args = (y, w3, x)
    body = functools.partial(_out_proj_kernel, alpha=alpha)
    if ln_stats is not None:
        row_stat = pl.BlockSpec((tm, 1), lambda i, j: (i, 0))
        col_vec = pl.BlockSpec((1, tn), lambda i, j: (0, j))
        in_specs += [row_stat, row_stat, col_vec, col_vec]
        args += tuple(ln_stats)
        body = functools.partial(_out_proj_ln_res_kernel, alpha=alpha)
    return pl.pallas_call(
        body,
        grid=(m // tm, n // tn),
        in_specs=in_specs,
        out_specs=pl.BlockSpec((tm, tn), lambda i, j: (i, j)),
        out_shape=jax.ShapeDtypeStruct((m, n), F32),
        compiler_params=pltpu.CompilerParams(
            dimension_semantics=("parallel", "parallel"), vmem_limit_bytes=VMEM_LIMIT),
        name="out_proj",
    )(*args)


def _layer_norm(h, g, b):
    mu = jnp.mean(h, axis=-1, keepdims=True)
    c = h - mu
    var = jnp.mean(c * c, axis=-1, keepdims=True)
    return c * lax.rsqrt(var + LN_EPS) * g + b


def _ln_kernel(h_ref, g_ref, b_ref, o_ref):
    o_ref[...] = _layer_norm(h_ref[...], g_ref[...], b_ref[...])


def _ln_gate_kernel(h_ref, g_ref, b_ref, w_ref, obf_ref, ab_ref, mu_ref, rstd_ref):
    h = h_ref[...]
    mu = jnp.mean(h, axis=-1, keepdims=True)
    c = h - mu
    rstd = lax.rsqrt(jnp.mean(c * c, axis=-1, keepdims=True) + LN_EPS)
    out = c * rstd * g_ref[...] + b_ref[...]
    obf_ref[...] = out.astype(BF16)
    mu_ref[...] = mu
    rstd_ref[...] = rstd
    xh, xl = _split_bf16(out)
    wh, wl = _split_bf16(w_ref[...])
    dot = functools.partial(jnp.dot, preferred_element_type=F32)
    ab_ref[...] = dot(xh, wh) + (dot(xl, wh) + dot(xh, wl))


def _ln(h, g, b, tm):
    m, n = h.shape
    tm = min(tm, m)
    row = pl.BlockSpec((tm, n), lambda i: (i, 0))
    vec = pl.BlockSpec((1, n), lambda i: (0, 0))
    return pl.pallas_call(
        _ln_kernel, grid=(m // tm,), in_specs=[row, vec, vec], out_specs=row,
        out_shape=jax.ShapeDtypeStruct((m, n), F32),
        compiler_params=pltpu.CompilerParams(
            dimension_semantics=("parallel",), vmem_limit_bytes=VMEM_LIMIT),
        name="layer_norm",
    )(h, g, b)


def _ln_gate(h, g, b, w_gate, tm):
    m, n = h.shape
    ng = w_gate.shape[1]
    tm = min(tm, m)
    row = pl.BlockSpec((tm, n), lambda i: (i, 0))
    vec = pl.BlockSpec((1, n), lambda i: (0, 0))
    stat = pl.BlockSpec((tm, 1), lambda i: (i, 0))
    return pl.pallas_call(
        _ln_gate_kernel, grid=(m // tm,),
        in_specs=[row, vec, vec, pl.BlockSpec((n, ng), lambda i: (0, 0))],
        out_specs=[row, pl.BlockSpec((tm, ng), lambda i: (i, 0)), stat, stat],
        out_shape=[jax.ShapeDtypeStruct((m, n), BF16), jax.ShapeDtypeStruct((m, ng), F32),
                   jax.ShapeDtypeStruct((m, 1), F32), jax.ShapeDtypeStruct((m, 1), F32)],
        compiler_params=pltpu.CompilerParams(
            dimension_semantics=("parallel",), vmem_limit_bytes=VMEM_LIMIT),
        name="layer_norm_gate",
    )(h, g, b, w_gate)


def _row_of(ref, h, base, s):
    return ref[h, pl.ds(base + s, 1), :]


def _hgrn2_offdiag(q, k, cum2):
    offs = []
    for blk in range(1, CHUNK // SUB):
        r0 = blk * SUB
        start = cum2[r0 - 1:r0]
        q_s = q[r0:r0 + SUB] * jnp.exp2(cum2[r0:r0 + SUB] - start)
        k_s = jnp.concatenate([k[:r0] * jnp.exp2(start - cum2[:r0]),
                               jnp.zeros((CHUNK - r0, HEAD_DIM), F32)], axis=0)
        offs.append(_dot_nt(q_s, k_s))
    return offs


def _hgrn2_intra(q, cum2, offs, k_rows, cum_rows):
    lane = lax.broadcasted_iota(jnp.int32, (SUB, CHUNK), 1)
    row = lax.broadcasted_iota(jnp.int32, (SUB, CHUNK), 0)
    blocks = []
    for blk in range(CHUNK // SUB):
        r0 = blk * SUB
        q_b, cum_b = q[r0:r0 + SUB], cum2[r0:r0 + SUB]
        a = jnp.zeros((SUB, CHUNK), F32)
        for j in range(SUB):
            s = r0 + j
            col = jnp.sum(q_b * (k_rows(s) * jnp.exp2(cum_b - cum_rows(s))), axis=-1, keepdims=True)
            a = jnp.where(lane == s, col, a)
        a = jnp.where(lane <= row + r0, a, 0.0)
        if blk > 0:
            a = jnp.where(lane < r0, offs[blk - 1], a)
        blocks.append(a)
    return jnp.concatenate(blocks, axis=0)


def _hgrn2_kernel(q_ref, f_ref, i_ref, z_ref, lbp_ref, nw_ref, tri_ref, y_ref, st_ref,
                  kbuf_ref, cbuf_ref, *, hb, t_blk, layer):
    @pl.when(pl.program_id(2) == 0)
    def _():
        st_ref[...] = jnp.zeros_like(st_ref)

    tri = tri_ref[...]
    nw = nw_ref[...]
    hrange = range(hb)
    q, k, v, cum2 = [], [], [], []
    for h in hrange:
        cols = slice(h * HEAD_DIM, (h + 1) * HEAD_DIM)
        lbp = lbp_ref[:, cols]
        ex = jnp.exp(lbp - jnp.max(lbp, axis=0, keepdims=True))
        lb = jnp.sum(ex[:layer + 1], axis=0, keepdims=True) / jnp.sum(ex, axis=0, keepdims=True)

        q.append(_silu(q_ref[:, cols]))
        forget = lb + (1.0 - lb) * jax.nn.sigmoid(f_ref[:, cols])
        k.append(1.0 - forget)
        v.append(i_ref[:, cols])

        g_hi, g_lo = _split_bf16(jnp.log2(forget))
        cs = jnp.dot(tri, jnp.concatenate([g_hi, g_lo], axis=1), preferred_element_type=F32)
        cum2.append(cs[:, :HEAD_DIM] + cs[:, HEAD_DIM:])
        kbuf_ref[h] = k[h]
        cbuf_ref[h] = cum2[h]

    st = [st_ref[h] for h in hrange]
    outs = [[] for _ in hrange]
    n_chunks = t_blk // CHUNK
    chunk_rows = [slice(c * CHUNK, (c + 1) * CHUNK) for c in range(n_chunks)]
    offs = {(c, h): _hgrn2_offdiag(q[h][chunk_rows[c]], k[h][chunk_rows[c]], cum2[h][chunk_rows[c]])
            for c in range(n_chunks) for h in hrange}
    for c in range(n_chunks):
        rows = chunk_rows[c]
        for h in hrange:
            q_c, k_c, v_c, cum_c = q[h][rows], k[h][rows], v[h][rows], cum2[h][rows]
            last = cum_c[CHUNK - 1:CHUNK]
            k_rows = functools.partial(_row_of, kbuf_ref, h, c * CHUNK)
            cum_rows = functools.partial(_row_of, cbuf_ref, h, c * CHUNK)
            a = _hgrn2_intra(q_c, cum_c, offs[c, h], k_rows, cum_rows)
            outs[h].append(_dot(a, v_c) + _dot_nt(q_c * jnp.exp2(cum_c), st[h]))
            st[h] = jnp.exp2(last) * st[h] + _dot_tn(v_c, k_c * jnp.exp2(last - cum_c))
    for h in hrange:
        cols = slice(h * HEAD_DIM, (h + 1) * HEAD_DIM)
        st_ref[h] = st[h]
        o_all = jnp.concatenate(outs[h], axis=0)
        y_ref[:, cols] = _gated_rmsnorm(o_all, z_ref[:, cols], nw).astype(y_ref.dtype)


def _block_diag_tri(t_blk):
    r = lax.broadcasted_iota(jnp.int32, (t_blk, t_blk), 0)
    c = lax.broadcasted_iota(jnp.int32, (t_blk, t_blk), 1)
    return ((c <= r) & (c // CHUNK == r // CHUNK)).astype(BF16)


def _hgrn2_scan(hproj, lb_params, norm_w, batch, seq, d, layer, hb, t_blk):
    hb = min(hb, d // HEAD_DIM)
    t_blk = min(t_blk, seq)
    w = hb * HEAD_DIM
    nhb = d // w
    nt = seq // t_blk

    def col_spec(part):
        return pl.BlockSpec((t_blk, w), lambda b, h, t: (b * nt + t, part * nhb + h))

    return pl.pallas_call(
        functools.partial(_hgrn2_kernel, hb=hb, t_blk=t_blk, layer=layer),
        grid=(batch, nhb, nt),
        in_specs=[col_spec(0), col_spec(1), col_spec(2), col_spec(3),
                  pl.BlockSpec((lb_params.shape[0], w), lambda b, h, t: (0, h)),
                  pl.BlockSpec((1, HEAD_DIM), lambda b, h, t: (0, 0)),
                  pl.BlockSpec((t_blk, t_blk), lambda b, h, t: (0, 0))],
        out_specs=pl.BlockSpec((t_blk, w), lambda b, h, t: (b * nt + t, h)),
        out_shape=jax.ShapeDtypeStruct((batch * seq, d), BF16),
        scratch_shapes=[pltpu.VMEM((hb, HEAD_DIM, HEAD_DIM), F32),
                        pltpu.VMEM((hb, t_blk, HEAD_DIM), F32),
                        pltpu.VMEM((hb, t_blk, HEAD_DIM), F32)],
        compiler_params=pltpu.CompilerParams(
            dimension_semantics=("parallel", "parallel", "arbitrary"),
            vmem_limit_bytes=VMEM_LIMIT),
        name="hgrn2_scan",
    )(hproj, hproj, hproj, hproj, lb_params, norm_w, _block_diag_tri(t_blk))


def _causal_conv_silu(buf_ref, x, w, t_blk):
    buf_ref[pl.ds(SUBLANES, t_blk), :] = x
    acc = x * w[CONV_TAPS - 1:CONV_TAPS]
    for j in range(1, CONV_TAPS):
        acc = acc + buf_ref[pl.ds(SUBLANES - j, t_blk), :] * w[CONV_TAPS - 1 - j:CONV_TAPS - j]
    buf_ref[pl.ds(0, SUBLANES), :] = x[t_blk - SUBLANES:]
    return _silu(acc)


def _l2_normalize(x):
    return x * lax.rsqrt(jnp.sum(x * x, axis=-1, keepdims=True) + L2_EPS)


def _gdn_kernel(alog_ref, dtb_ref, q_ref, k_ref, v_ref, z_ref, cwq_ref, cwk_ref, cwv_ref,
                ab_ref, nw_ref, y_ref, s_ref, stage_ref, *, hb, t_blk, heads):
    @pl.when(pl.program_id(2) == 0)
    def _():
        s_ref[...] = jnp.zeros_like(s_ref)
        stage_ref[:, :, pl.ds(0, SUBLANES), :] = jnp.zeros((3, hb, SUBLANES, HEAD_DIM), F32)

    nw = nw_ref[...]
    r = lax.broadcasted_iota(jnp.int32, (GROUP, GROUP), 0)
    c = lax.broadcasted_iota(jnp.int32, (GROUP, GROUP), 1)
    same_chunk = (r // CHUNK) == (c // CHUNK)
    eye = r == c
    incl = same_chunk & (c <= r)
    strict = same_chunk & (c < r)
    chunks_per_group = GROUP // CHUNK
    n_groups = t_blk // GROUP
    hrange = range(hb)
    problems = [(h, grp) for h in hrange for grp in range(n_groups)]

    q_all, k_all, v_all = [], [], []
    for h in hrange:
        cols = slice(h * HEAD_DIM, (h + 1) * HEAD_DIM)
        streams = []
        for idx, (x_ref, cw_ref) in enumerate(((q_ref, cwq_ref), (k_ref, cwk_ref), (v_ref, cwv_ref))):
            streams.append(_causal_conv_silu(stage_ref.at[idx, h], x_ref[:, cols],
                                             cw_ref[:, cols], t_blk))
        q_all.append(_l2_normalize(streams[0]) * (HEAD_DIM ** -0.5))
        k_all.append(_l2_normalize(streams[1]))
        v_all.append(streams[2])

    def rows_of(grp):
        return slice(grp * GROUP, (grp + 1) * GROUP)

    kk = {p: _dot_nt(k_all[p[0]][rows_of(p[1])], k_all[p[0]][rows_of(p[1])]) for p in problems}
    qk = {p: _dot_nt(q_all[p[0]][rows_of(p[1])], k_all[p[0]][rows_of(p[1])]) for p in problems}

    cum_col, e_cum, rhs, inv, power = {}, {}, {}, {}, {}
    for p in problems:
        h, grp = p
        head = pl.program_id(1) * hb + h
        neg_a = -jnp.exp(jnp.full((1, GROUP), alog_ref[head], F32))
        beta_row = jax.nn.sigmoid(ab_ref[0, grp, pl.ds(head, 1), :])
        a_row = ab_ref[0, grp, pl.ds(heads + head, 1), :] + jnp.full((1, GROUP), dtb_ref[head], F32)
        g_row = neg_a * (jnp.maximum(a_row, 0.0) + jnp.log1p(jnp.exp(-jnp.abs(a_row))))

        cum_col[p] = jnp.sum(jnp.where(incl, g_row, 0.0), axis=1, keepdims=True)
        cum_row = jnp.sum(jnp.where(eye, cum_col[p], 0.0), axis=0, keepdims=True)
        beta_col = jnp.sum(jnp.where(eye, beta_row, 0.0), axis=1, keepdims=True)

        decay = jnp.exp(jnp.minimum(cum_col[p] - cum_row, 0.0))
        qk[p] = qk[p] * jnp.where(incl, decay, 0.0)
        power[p] = -(beta_col * kk[p]) * jnp.where(strict, decay, 0.0)
        inv[p] = jnp.where(eye, 1.0, 0.0) + power[p]
        e_cum[p] = jnp.exp(cum_col[p])
        rhs[p] = jnp.concatenate([beta_col * v_all[h][rows_of(grp)],
                                  (beta_col * e_cum[p]) * k_all[h][rows_of(grp)]], axis=1)

    for _ in range(5):
        for p in problems:
            power[p] = _dot(power[p], power[p])
        for p in problems:
            inv[p] = inv[p] + _dot(inv[p], power[p])
    sol = {p: _dot(inv[p], rhs[p]) for p in problems}

    state = [s_ref[h] for h in hrange]
    outs = [[] for _ in hrange]
    for grp in range(n_groups):
        u_parts = [[] for _ in hrange]
        for ci in range(chunks_per_group):
            cr = slice(ci * CHUNK, (ci + 1) * CHUNK)
            last_row = (ci + 1) * CHUNK - 1
            ws = []
            for h in hrange:
                p = (h, grp)
                q_dec = q_all[h][rows_of(grp)][cr] * e_cum[p][cr]
                ws.append(_dot(jnp.concatenate([sol[p][cr, HEAD_DIM:], q_dec], axis=0), state[h]))
            u = [sol[(h, grp)][cr, :HEAD_DIM] - ws[h][:CHUNK] for h in hrange]
            for h in hrange:
                p = (h, grp)
                last = cum_col[p][last_row:last_row + 1]
                k_dec = k_all[h][rows_of(grp)][cr] * jnp.exp(last - cum_col[p][cr])
                state[h] = jnp.exp(last) * state[h] + _dot_tn(k_dec, u[h])
            for h in hrange:
                p = (h, grp)
                u_parts[h].append(u[h])
                rest = [sol[p][(ci + 1) * CHUNK:, :HEAD_DIM]] if ci + 1 < chunks_per_group else []
                u_full = jnp.concatenate(u_parts[h] + rest, axis=0)
                outs[h].append(ws[h][CHUNK:] + _dot(qk[p][cr], u_full))
    for h in hrange:
        cols = slice(h * HEAD_DIM, (h + 1) * HEAD_DIM)
        s_ref[h] = state[h]
        o_all = jnp.concatenate(outs[h], axis=0)
        y_ref[:, cols] = _gated_rmsnorm(o_all, z_ref[:, cols], nw).astype(y_ref.dtype)


def _gdn_scan(hproj, conv_w, ab_rows, a_log, dt_bias, norm_w, batch, seq, d, hb, t_blk):
    heads = d // HEAD_DIM
    hb = min(hb, heads)
    t_blk = min(t_blk, seq)
    w = hb * HEAD_DIM
    nhb = d // w
    nt = seq // t_blk
    gpb = t_blk // GROUP

    def col_spec(part):
        return pl.BlockSpec((t_blk, w), lambda b, h, t: (b * nt + t, part * nhb + h))

    def conv_spec(part):
        return pl.BlockSpec((CONV_TAPS, w), lambda b, h, t: (0, part * nhb + h))

    smem = pl.BlockSpec(memory_space=pltpu.SMEM)
    return pl.pallas_call(
        functools.partial(_gdn_kernel, hb=hb, t_blk=t_blk, heads=heads),
        grid=(batch, nhb, nt),
        in_specs=[smem, smem, col_spec(0), col_spec(1), col_spec(2), col_spec(3),
                  conv_spec(0), conv_spec(1), conv_spec(2),
                  pl.BlockSpec((1, gpb, 2 * heads, GROUP), lambda b, h, t: (b, t, 0, 0)),
                  pl.BlockSpec((1, HEAD_DIM), lambda b, h, t: (0, 0))],
        out_specs=pl.BlockSpec((t_blk, w), lambda b, h, t: (b * nt + t, h)),
        out_shape=jax.ShapeDtypeStruct((batch * seq, d), BF16),
        scratch_shapes=[pltpu.VMEM((hb, HEAD_DIM, HEAD_DIM), F32),
                        pltpu.VMEM((3, hb, SUBLANES + t_blk, HEAD_DIM), F32)],
        compiler_params=pltpu.CompilerParams(
            dimension_semantics=("parallel", "parallel", "arbitrary"),
            vmem_limit_bytes=VMEM_LIMIT),
        name="gdn_scan",
    )(a_log, dt_bias, hproj, hproj, hproj, hproj, conv_w, conv_w, conv_w, ab_rows, norm_w)


def kernel(x, a_w_in, a_lower_bounds, a_norm_w, a_w_out, b_w_in, b_conv_w, b_a_log, b_dt_bias,
           b_norm_w, b_w_out, ln_g, ln_b):
    batch, seq, d = x.shape
    heads = d // HEAD_DIM
    depth = ln_g.shape[0]
    assert depth == 2 and a_w_in.shape[0] == 1 and b_w_in.shape[0] == 1
    assert a_w_in.shape[2] == 4 * d and b_w_in.shape[2] == 4 * d + 2 * heads
    assert seq % GROUP == 0 and d % HEAD_DIM == 0
    alpha = (2.0 * depth) ** 0.25
    m = batch * seq
    x2 = x.reshape(m, d)

    hproj = _matmul(x2.astype(BF16), a_w_in, 4 * d, tm=2048, tn=512)
    y = _hgrn2_scan(hproj, a_lower_bounds, a_norm_w, batch, seq, d, layer=0, hb=16, t_blk=256)
    h = _out_proj(y, a_w_out, x2, alpha, tm=2048, tn=512)
    b_w_in_t = jnp.swapaxes(b_w_in, 1, 2)
    w_gate = jnp.pad(b_w_in_t[0, 4 * d:, :].T, ((0, 0), (0, HEAD_DIM - 2 * heads)))
    x2_bf, ab, mu, rstd = _ln_gate(h, ln_g[0:1], ln_b[0:1], w_gate, tm=256)

    hproj = _matmul(x2_bf, b_w_in_t, 4 * d, tm=2048, tn=512, w_transposed=True)
    ab_rows = ab[:, :2 * heads].reshape(batch, seq // GROUP, GROUP, 2 * heads).transpose(0, 1, 3, 2)
    y = _gdn_scan(hproj, b_conv_w[0], ab_rows, b_a_log[0], b_dt_bias[0], b_norm_w,
                  batch, seq, d, hb=8, t_blk=256)
    h = _out_proj(y, b_w_out, h, alpha, tm=1024, tn=512, ln_stats=(mu, rstd, ln_g[0:1], ln_b[0:1]))
    return _ln(h, ln_g[1:2], ln_b[1:2], tm=256).reshape(batch, seq, d)
```
